```python
import math
import jax, jax.numpy as jnp
from jax import lax
import numpy as np

D_MODEL = 1024
BATCH = 8
SEQ = 2048
DEPTH = 2
DEC_BATCH = 128
DEC_SEQ = 1
PAST_LEN = 16384
PAGE_SIZE = 128

N_MIXERS = 2
N_S5_LAYERS = (DEPTH + 1) // 2
N_ML_LAYERS = DEPTH // 2

S5_WIDTH = D_MODEL
S5_GROUP = 16
S5_GROUPS = S5_WIDTH // S5_GROUP
S5_STATE = 64
S5_CHUNK = 128

ML_INNER = 2 * D_MODEL
ML_HEADS = 4
ML_DH = ML_INNER // ML_HEADS
ML_CONV = 4
ML_CHUNK = 64

N_EXPERTS = 32
TOP_K = 4
D_EXPERT = D_MODEL
SWIGLU_LIMIT = 7.0
SWIGLU_ALPHA = 1.702
MOE_BLOCK = 128

EPS = 1e-6

kernel_name = "s5_mlstm_moe_hybrid_step"


def rmsnorm(x, g):
    xf = x.astype(jnp.float32)
    y = xf * lax.rsqrt(jnp.mean(xf * xf, axis=-1, keepdims=True) + EPS)
    return (y * g.astype(jnp.float32)).astype(x.dtype)


def ada_mod(c, w_mod, b_mod):
    m = jax.nn.silu(c) @ w_mod + b_mod
    return [t[:, None, :] for t in jnp.split(m, 6, axis=-1)]


def s5_discretise(lam_re, lam_im, log_dt, b_re, b_im):
    f32 = jnp.float32
    dt = jnp.exp(log_dt.astype(f32))[:, None]
    lr, li = lam_re.astype(f32), lam_im.astype(f32)
    mag = jnp.exp(lr * dt)
    ar, ai = mag * jnp.cos(li * dt), mag * jnp.sin(li * dt)
    den = lr * lr + li * li
    nr = ar - 1.0
    wr = (nr * lr + ai * li) / den
    wi = (ai * lr - nr * li) / den
    br, bi = b_re.astype(f32), b_im.astype(f32)
    bbr = wr[..., None] * br - wi[..., None] * bi
    bbi = wr[..., None] * bi + wi[..., None] * br
    return ar, ai, bbr, bbi


def _s5_combine(e1, e2):
    a1r, a1i, b1r, b1i = e1
    a2r, a2i, b2r, b2i = e2
    return (a2r * a1r - a2i * a1i, a2r * a1i + a2i * a1r,
            a2r * b1r - a2i * b1i + b2r, a2r * b1i + a2i * b1r + b2i)


def s5_scan(bu_re, bu_im, ar, ai, s0_re, s0_im):
    nb, L = bu_re.shape[:2]
    ch = math.gcd(L, S5_CHUNK)
    nc = L // ch

    def to_chunks(t):
        return t.reshape(nb, nc, ch, *t.shape[2:]).swapaxes(0, 1)

    def step(carry, inp):
        sr, si = carry
        br, bi = inp
        a_r = jnp.broadcast_to(ar, br.shape)
        a_i = jnp.broadcast_to(ai, br.shape)
        pr, pi, hr, hi = lax.associative_scan(_s5_combine, (a_r, a_i, br, bi), axis=1)
        s_r = hr + pr * sr[:, None] - pi * si[:, None]
        s_i = hi + pr * si[:, None] + pi * sr[:, None]
        return (s_r[:, -1], s_i[:, -1]), (s_r, s_i)

    (fr, fi), (sr, si) = lax.scan(step, (s0_re, s0_im), (to_chunks(bu_re), to_chunks(bu_im)))

    def back(t):
        return t.swapaxes(0, 1).reshape(nb, L, *t.shape[3:])

    return back(sr), back(si), fr, fi


def s5_mixer(h, s0_re, s0_im, w_in, lam_re, lam_im, log_dt, b_re, b_im, c_re, c_im, d_skip, w_glu):
    f32 = jnp.float32
    nb, L, _ = h.shape
    u = (h @ w_in).astype(f32).reshape(nb, L, S5_GROUPS, S5_GROUP)
    ar, ai, bbr, bbi = s5_discretise(lam_re, lam_im, log_dt, b_re, b_im)
    bu_re = jnp.einsum('blgc,gpc->blgp', u, bbr)
    bu_im = jnp.einsum('blgc,gpc->blgp', u, bbi)
    sr, si, fr, fi = s5_scan(bu_re, bu_im, ar, ai, s0_re.astype(f32), s0_im.astype(f32))
    y = (jnp.einsum('blgp,gcp->blgc', sr, c_re.astype(f32))
         - jnp.einsum('blgp,gcp->blgc', si, c_im.astype(f32)))
    y = y.reshape(nb, L, S5_WIDTH) + d_skip.astype(f32) * u.reshape(nb, L, S5_WIDTH)
    y = jax.nn.gelu(y).astype(h.dtype)
    val, gate = jnp.split(y @ w_glu, 2, axis=-1)
    return val * jax.nn.sigmoid(gate), fr, fi


def causal_conv(xm, buf, w, b):
    L = xm.shape[1]
    xp = jnp.concatenate([buf.astype(xm.dtype), xm], axis=1)
    y = b
    for k in range(ML_CONV):
        y = y + w[k] * xp[:, k:k + L]
    return y, xp[:, -(ML_CONV - 1):]


def mlstm_chunked(q, k, v, ig, lf, C0, n0, m0):
    nb, L = q.shape[:2]
    ch = math.gcd(L, ML_CHUNK)
    nc = L // ch
    causal = jnp.tril(jnp.ones((ch, ch), bool))[None, :, :, None]

    def to_chunks(t):
        return t.reshape(nb, nc, ch, *t.shape[2:]).swapaxes(0, 1)

    def step(carry, inp):
        C, n, m = carry
        qc, kc, vc, igc, lfc = inp
        b = jnp.cumsum(lfc, axis=1)
        dm = b[:, :, None, :] - b[:, None, :, :] + igc[:, None, :, :]
        dm = jnp.where(causal, dm, -jnp.inf)
        inter = b + m[:, None, :]
        m_t = jnp.maximum(inter, jnp.max(dm, axis=2))
        w_intra = jnp.exp(dm - m_t[:, :, None, :])
        w_inter = jnp.exp(inter - m_t)
        qk = jnp.einsum('bthd,bshd->btsh', qc, kc) * w_intra
        num = (jnp.einsum('btsh,bshd->bthd', qk, vc)
               + w_inter[..., None] * jnp.einsum('bthk,bhkv->bthv', qc, C))
        den = jnp.sum(qk, axis=2) + w_inter * jnp.einsum('bthk,bhk->bth', qc, n)
        h = num / jnp.maximum(jnp.abs(den), jnp.exp(-m_t))[..., None]
        bl = b[:, -1]
        m_new = m_t[:, -1]
        w_s = jnp.exp(bl[:, None] - b + igc - m_new[:, None])
        decay = jnp.exp(bl + m - m_new)
        C_new = decay[..., None, None] * C + jnp.einsum('bsh,bshk,bshv->bhkv', w_s, kc, vc)
        n_new = decay[..., None] * n + jnp.einsum('bsh,bshk->bhk', w_s, kc)
        return (C_new, n_new, m_new), h

    (Cf, nf, mf), h = lax.scan(step, (C0, n0, m0),
                               (to_chunks(q), to_chunks(k), to_chunks(v), to_chunks(ig), to_chunks(lf)))
    h = h.swapaxes(0, 1).reshape(nb, L, ML_HEADS, ML_DH)
    return h, Cf, nf, mf


def mlstm_mixer(h, C0, n0, m0, buf, w_up, conv_w, conv_b, w_q, w_k, w_v, w_gate, b_gate, ln_w, skip, w_down):
    f32 = jnp.float32
    nb, L, _ = h.shape
    xm, z = jnp.split(h @ w_up, 2, axis=-1)
    xc, new_buf = causal_conv(xm, buf, conv_w, conv_b)
    xc = jax.nn.silu(xc)
    xch = xc.reshape(nb, L, ML_HEADS, ML_DH)
    xmh = xm.reshape(nb, L, ML_HEADS, ML_DH)
    q = jnp.einsum('blhd,hde->blhe', xch, w_q)
    k = jnp.einsum('blhd,hde->blhe', xch, w_k) * (ML_DH ** -0.5)
    v = jnp.einsum('blhd,hde->blhe', xmh, w_v)
    qkv = jnp.concatenate([q.reshape(nb, L, ML_INNER), k.reshape(nb, L, ML_INNER),
                           v.reshape(nb, L, ML_INNER)], axis=-1)
    g = (qkv @ w_gate + b_gate).astype(f32)
    ig, fg = jnp.split(g, 2, axis=-1)
    lf = jax.nn.log_sigmoid(fg)
    hh, Cf, nf, mf = mlstm_chunked(q.astype(f32), k.astype(f32), v.astype(f32), ig, lf,
                                   C0.astype(f32), n0.astype(f32), m0.astype(f32))
    mu = jnp.mean(hh, axis=-1, keepdims=True)
    var = jnp.mean(jnp.square(hh - mu), axis=-1, keepdims=True)
    hn = ((hh - mu) * lax.rsqrt(var + EPS)).reshape(nb, L, ML_INNER) * ln_w.astype(f32)
    out = ((hn + skip.astype(f32) * xc.astype(f32)) * jax.nn.silu(z.astype(f32))).astype(h.dtype)
    return out @ w_down, Cf, nf, mf, new_buf


def moe(x, w_router, b_router, w_gu, b_gu, w_down, b_down):
    T, D = x.shape
    logits = (x @ w_router + b_router).astype(jnp.float32)
    top_v, top_i = lax.top_k(logits, TOP_K)
    gates = jax.nn.softmax(top_v, axis=-1)
    A = T * TOP_K
    e_flat = top_i.reshape(-1)
    tok_flat = jnp.repeat(jnp.arange(T, dtype=jnp.int32), TOP_K)
    g_flat = gates.reshape(-1)
    counts = jnp.zeros((N_EXPERTS,), jnp.int32).at[e_flat].add(1)
    padded = (counts + MOE_BLOCK - 1) // MOE_BLOCK * MOE_BLOCK
    pad_end = jnp.cumsum(padded)
    pad_start = pad_end - padded
    cnt_start = jnp.cumsum(counts) - counts
    order = jnp.argsort(e_flat)
    e_sorted = e_flat[order]
    dest = pad_start[e_sorted] + (jnp.arange(A, dtype=jnp.int32) - cnt_start[e_sorted])
    n_blocks = -(-A // MOE_BLOCK) + N_EXPERTS
    n_slots = n_blocks * MOE_BLOCK
    slot_tok = jnp.full((n_slots,), T, jnp.int32).at[dest].set(tok_flat[order])
    slot_gate = jnp.zeros((n_slots,), jnp.float32).at[dest].set(g_flat[order])
    blk_start = jnp.arange(n_blocks, dtype=jnp.int32) * MOE_BLOCK
    blk_expert = jnp.minimum(jnp.searchsorted(pad_end, blk_start, side='right'), N_EXPERTS - 1)
    x_pad = jnp.concatenate([x, jnp.zeros((1, D), x.dtype)], axis=0)

    def run_block(args):
        toks, e = args
        xb = x_pad[toks]
        x_glu, x_lin = jnp.split(xb @ w_gu[e] + b_gu[e], 2, axis=-1)
        x_glu = jnp.minimum(x_glu, SWIGLU_LIMIT)
        x_lin = jnp.clip(x_lin, -SWIGLU_LIMIT, SWIGLU_LIMIT)
        act = x_glu * jax.nn.sigmoid(SWIGLU_ALPHA * x_glu) * (x_lin + 1.0)
        return act @ w_down[e] + b_down[e]

    outs = lax.map(run_block, (slot_tok.reshape(n_blocks, MOE_BLOCK), blk_expert))
    outs = outs.reshape(n_slots, D) * slot_gate[:, None]
    y = jax.ops.segment_sum(outs, slot_tok, num_segments=T + 1)[:T]
    return y.astype(x.dtype)


def run_group(x, c, s5_re, s5_im, m_C, m_n, m_m, m_conv, params):
    (norm_mix_g, norm_ffn_g, final_norm_g, ada_w, ada_b,
     s5_w_in, s5_lam_re, s5_lam_im, s5_log_dt, s5_b_re, s5_b_im, s5_c_re, s5_c_im, s5_d, s5_w_glu,
     ml_w_up, ml_conv_w, ml_conv_b, ml_w_q, ml_w_k, ml_w_v, ml_w_gate, ml_b_gate, ml_ln_w, ml_skip, ml_w_down,
     moe_w_router, moe_b_router, moe_w_gu, moe_b_gu, moe_w_down, moe_b_down) = params
    new_re, new_im, new_C, new_n, new_m, new_conv = [], [], [], [], [], []
    for i in range(DEPTH):
        sh_m, sc_m, g_m, sh_f, sc_f, g_f = ada_mod(c, ada_w[i], ada_b[i])
        h = rmsnorm(x, norm_mix_g[i]) * (1.0 + sc_m) + sh_m
        j = i // N_MIXERS
        if i % N_MIXERS == 0:
            out, fr, fi = s5_mixer(h, s5_re[j], s5_im[j], s5_w_in[j], s5_lam_re[j], s5_lam_im[j],
                                   s5_log_dt[j], s5_b_re[j], s5_b_im[j], s5_c_re[j], s5_c_im[j],
                                   s5_d[j], s5_w_glu[j])
            new_re.append(fr)
            new_im.append(fi)
        else:
            out, Cf, nf, mf, bf = mlstm_mixer(h, m_C[j], m_n[j], m_m[j], m_conv[j], ml_w_up[j],
                                              ml_conv_w[j], ml_conv_b[j], ml_w_q[j], ml_w_k[j],
                                              ml_w_v[j], ml_w_gate[j], ml_b_gate[j], ml_ln_w[j],
                                              ml_skip[j], ml_w_down[j])
            new_C.append(Cf)
            new_n.append(nf)
            new_m.append(mf)
            new_conv.append(bf)
        x = x + g_m * out
        h = rmsnorm(x, norm_ffn_g[i]) * (1.0 + sc_f) + sh_f
        nb, L, D = h.shape
        f = moe(h.reshape(nb * L, D), moe_w_router[i], moe_b_router[i], moe_w_gu[i], moe_b_gu[i],
                moe_w_down[i], moe_b_down[i]).reshape(nb, L, D)
        x = x + g_f * f
    y = rmsnorm(x, final_norm_g)
    return (y, jnp.stack(new_re), jnp.stack(new_im), jnp.stack(new_C), jnp.stack(new_n),
            jnp.stack(new_m), jnp.stack(new_conv))


def setup_inputs(seed: int = 0) -> dict:
    key = jax.random.key(seed)
    ks = iter(jax.random.split(key, 64))
    f32 = jnp.float32

    def nrm(shape, s):
        return s * jax.random.normal(next(ks), shape, f32)

    def gain(shape):
        return 1.0 + nrm(shape, 0.02)

    D = D_MODEL
    lam_im = (math.pi * jnp.arange(S5_STATE, dtype=f32))[None, None, :] + nrm((N_S5_LAYERS, S5_GROUPS, S5_STATE), 0.01)
    b_gate = jnp.concatenate([nrm((N_ML_LAYERS, ML_HEADS), 0.1),
                              jnp.linspace(3.0, 6.0, ML_HEADS, dtype=f32)[None, :]
                              + nrm((N_ML_LAYERS, ML_HEADS), 0.01)], axis=-1)
    return {
        "x_prompt": nrm((BATCH, SEQ, D), 1.0),
        "x_sample": nrm((DEC_BATCH, DEC_SEQ, D), 1.0),
        "c_prompt": nrm((BATCH, D), 1.0),
        "c_sample": nrm((DEC_BATCH, D), 1.0),
        "state_s5_re": nrm((N_S5_LAYERS, DEC_BATCH, S5_GROUPS, S5_STATE), 0.1),
        "state_s5_im": nrm((N_S5_LAYERS, DEC_BATCH, S5_GROUPS, S5_STATE), 0.1),
        "state_mlstm_C": nrm((N_ML_LAYERS, DEC_BATCH, ML_HEADS, ML_DH, ML_DH), 0.05),
        "state_mlstm_n": nrm((N_ML_LAYERS, DEC_BATCH, ML_HEADS, ML_DH), 0.1),
        "state_mlstm_m": jax.random.uniform(next(ks), (N_ML_LAYERS, DEC_BATCH, ML_HEADS), f32, 0.0, 3.0),
        "state_mlstm_conv": nrm((N_ML_LAYERS, DEC_BATCH, ML_CONV - 1, ML_INNER), 1.0),
        "norm_mix_g": gain((DEPTH, D)),
        "norm_ffn_g": gain((DEPTH, D)),
        "final_norm_g": gain((D,)),
        "ada_w": nrm((DEPTH, D, 6 * D), 0.5 * D ** -0.5),
        "ada_b": nrm((DEPTH, 6 * D), 0.02),
        "s5_w_in": nrm((N_S5_LAYERS, D, S5_WIDTH), D ** -0.5),
        "s5_lam_re": -0.5 + nrm((N_S5_LAYERS, S5_GROUPS, S5_STATE), 0.01),
        "s5_lam_im": lam_im,
        "s5_log_dt": jax.random.uniform(next(ks), (N_S5_LAYERS, S5_GROUPS), f32,
                                        math.log(1e-3), math.log(1e-1)),
        "s5_b_re": nrm((N_S5_LAYERS, S5_GROUPS, S5_STATE, S5_GROUP), (2 * S5_GROUP) ** -0.5),
        "s5_b_im": nrm((N_S5_LAYERS, S5_GROUPS, S5_STATE, S5_GROUP), (2 * S5_GROUP) ** -0.5),
        "s5_c_re": nrm((N_S5_LAYERS, S5_GROUPS, S5_GROUP, S5_STATE), (2 * S5_STATE) ** -0.5),
        "s5_c_im": nrm((N_S5_LAYERS, S5_GROUPS, S5_GROUP, S5_STATE), (2 * S5_STATE) ** -0.5),
        "s5_d": nrm((N_S5_LAYERS, S5_WIDTH), 1.0),
        "s5_w_glu": nrm((N_S5_LAYERS, S5_WIDTH, 2 * D), S5_WIDTH ** -0.5),
        "ml_w_up": nrm((N_ML_LAYERS, D, 2 * ML_INNER), D ** -0.5),
        "ml_conv_w": nrm((N_ML_LAYERS, ML_CONV, ML_INNER), ML_CONV ** -0.5),
        "ml_conv_b": nrm((N_ML_LAYERS, ML_INNER), 0.01),
        "ml_w_q": nrm((N_ML_LAYERS, ML_HEADS, ML_DH, ML_DH), ML_DH ** -0.5),
        "ml_w_k": nrm((N_ML_LAYERS, ML_HEADS, ML_DH, ML_DH), ML_DH ** -0.5),
        "ml_w_v": nrm((N_ML_LAYERS, ML_HEADS, ML_DH, ML_DH), ML_DH ** -0.5),
        "ml_w_gate": nrm((N_ML_LAYERS, 3 * ML_INNER, 2 * ML_HEADS), 0.3 * (3 * ML_INNER) ** -0.5),
        "ml_b_gate": b_gate,
        "ml_ln_w": gain((N_ML_LAYERS, ML_INNER)),
        "ml_skip": gain((N_ML_LAYERS, ML_INNER)),
        "ml_w_down": nrm((N_ML_LAYERS, ML_INNER, D), ML_INNER ** -0.5),
        "moe_w_router": nrm((DEPTH, D, N_EXPERTS), D ** -0.5),
        "moe_b_router": nrm((DEPTH, N_EXPERTS), 0.01),
        "moe_w_gu": nrm((DEPTH, N_EXPERTS, D, 2 * D_EXPERT), D ** -0.5),
        "moe_b_gu": nrm((DEPTH, N_EXPERTS, 2 * D_EXPERT), 0.01),
        "moe_w_down": nrm((DEPTH, N_EXPERTS, D_EXPERT, D), D_EXPERT ** -0.5),
        "moe_b_down": nrm((DEPTH, N_EXPERTS, D), 0.01),
    }


def reference(x_prompt, x_sample, c_prompt, c_sample,
              state_s5_re, state_s5_im, state_mlstm_C, state_mlstm_n, state_mlstm_m, state_mlstm_conv,
              norm_mix_g, norm_ffn_g, final_norm_g, ada_w, ada_b,
              s5_w_in, s5_lam_re, s5_lam_im, s5_log_dt, s5_b_re, s5_b_im, s5_c_re, s5_c_im, s5_d, s5_w_glu,
              ml_w_up, ml_conv_w, ml_conv_b, ml_w_q, ml_w_k, ml_w_v, ml_w_gate, ml_b_gate, ml_ln_w, ml_skip,
              ml_w_down,
              moe_w_router, moe_b_router, moe_w_gu, moe_b_gu, moe_w_down, moe_b_down):
    params = (norm_mix_g, norm_ffn_g, final_norm_g, ada_w, ada_b,
              s5_w_in, s5_lam_re, s5_lam_im, s5_log_dt, s5_b_re, s5_b_im, s5_c_re, s5_c_im, s5_d, s5_w_glu,
              ml_w_up, ml_conv_w, ml_conv_b, ml_w_q, ml_w_k, ml_w_v, ml_w_gate, ml_b_gate, ml_ln_w, ml_skip,
              ml_w_down,
              moe_w_router, moe_b_router, moe_w_gu, moe_b_gu, moe_w_down, moe_b_down)
    f32 = jnp.float32
    bp = x_prompt.shape[0]
    (y_prompt, p_s5_re, p_s5_im, p_C, p_n, p_m, p_conv) = run_group(
        x_prompt, c_prompt,
        jnp.zeros((N_S5_LAYERS, bp, S5_GROUPS, S5_STATE), f32),
        jnp.zeros((N_S5_LAYERS, bp, S5_GROUPS, S5_STATE), f32),
        jnp.zeros((N_ML_LAYERS, bp, ML_HEADS, ML_DH, ML_DH), f32),
        jnp.zeros((N_ML_LAYERS, bp, ML_HEADS, ML_DH), f32),
        jnp.zeros((N_ML_LAYERS, bp, ML_HEADS), f32),
        jnp.zeros((N_ML_LAYERS, bp, ML_CONV - 1, ML_INNER), x_prompt.dtype),
        params)
    (y_sample, s_s5_re, s_s5_im, s_C, s_n, s_m, s_conv) = run_group(
        x_sample, c_sample, state_s5_re, state_s5_im, state_mlstm_C, state_mlstm_n, state_mlstm_m,
        state_mlstm_conv, params)
    return (y_prompt, y_sample, p_s5_re, p_s5_im, p_C, p_n, p_m, p_conv,
            s_s5_re, s_s5_im, s_C, s_n, s_m, s_conv)
```

```python
import functools

import jax
import jax.numpy as jnp
from jax import lax
from jax.experimental import pallas as pl
from jax.experimental.pallas import tpu as pltpu

F32 = jnp.float32
BF16 = jnp.bfloat16
EPS = 1e-6
TOP_K = 4
SWIGLU_LIMIT = 7.0
SWIGLU_ALPHA = 1.702
ML_CONV = 4

LANES = 128
SUBLANES = 8
MIB = 1024 * 1024

S5_CHUNK = 64
ML_CHUNK = 256
ROUTER_ROWS = 512
GMM_ROWS = 512
NORM_ROWS = 512


def _cparams(semantics, vmem_mib):
    return pltpu.CompilerParams(dimension_semantics=semantics, vmem_limit_bytes=int(vmem_mib * MIB))


def _dot(a, b):
    return jnp.dot(a, b, preferred_element_type=F32)


def _dot_nt(a, b):
    return lax.dot_general(a, b, (((1,), (1,)), ((), ())), preferred_element_type=F32)


def _norm_mod(x, g, sc, sh):
    y = x * lax.rsqrt(jnp.mean(x * x, axis=-1, keepdims=True) + EPS)
    return (y * g) * (1.0 + sc) + sh


def _silu(x):
    return x * jax.nn.sigmoid(x)


def _log_sigmoid(x):
    return -(jnp.maximum(-x, 0.0) + jnp.log1p(jnp.exp(-jnp.abs(x))))


def _ada_kernel(c_ref, w_ref, b_ref, o_ref):
    c = c_ref[...]
    o_ref[0] = _dot(_silu(c).astype(BF16), w_ref[0].astype(BF16)) + b_ref[0]


def _ada(c_all, ada_w, ada_b):
    depth, d, n = ada_w.shape
    rows = c_all.shape[0]
    tn = n // 4
    return pl.pallas_call(
        _ada_kernel,
        grid=(depth, n // tn),
        in_specs=[pl.BlockSpec((rows, d), lambda i, j: (0, 0)),
                  pl.BlockSpec((1, d, tn), lambda i, j: (i, 0, j)),
                  pl.BlockSpec((1, 1, tn), lambda i, j: (i, 0, j))],
        out_specs=pl.BlockSpec((1, rows, tn), lambda i, j: (i, 0, j)),
        out_shape=jax.ShapeDtypeStruct((depth, rows, n), F32),
        compiler_params=_cparams(("arbitrary", "arbitrary"), 32),
        name="ada",
    )(c_all, ada_w, ada_b.reshape(depth, 1, n))


def _s5_prep_kernel(lr_ref, li_ref, ldt_ref, br_ref, bi_ref, ar_ref, ai_ref, bbr_ref, bbi_ref):
    dt = jnp.exp(ldt_ref[...])
    lr = lr_ref[...]
    li = li_ref[...]
    mag = jnp.exp(lr * dt)
    ar = mag * jnp.cos(li * dt)
    ai = mag * jnp.sin(li * dt)
    den = lr * lr + li * li
    nr = ar - 1.0
    wr = (nr * lr + ai * li) / den
    wi = (ai * lr - nr * li) / den
    br = br_ref[...]
    bi = bi_ref[...]
    ar_ref[...] = ar
    ai_ref[...] = ai
    bbr_ref[...] = wr * br - wi * bi
    bbi_ref[...] = wr * bi + wi * br


def _s5_prep(lam_re, lam_im, log_dt, b_re, b_im):
    g, p, c = b_re.shape
    rep = lambda t: jnp.repeat(t, c, axis=0)
    bt = lambda t: t.transpose(0, 2, 1).reshape(g * c, p)
    shp = jax.ShapeDtypeStruct((g * c, p), F32)
    ar, ai, bbr, bbi = pl.pallas_call(
        _s5_prep_kernel, out_shape=(shp, shp, shp, shp), name="s5_prep",
    )(rep(lam_re), rep(lam_im), rep(log_dt[:, None]), bt(b_re), bt(b_im))
    return ar[::c], ai[::c], bbr, bbi


def _block_diag(t, nblk):
    g, a, b = t.shape
    gl = g // nblk
    t4 = t.reshape(nblk, gl, a, b)
    eye = jnp.eye(gl, dtype=t.dtype)
    return jnp.einsum('jgab,gh->jgahb', t4, eye).reshape(nblk, gl * a, gl * b)


def _s5_kernel(x_ref, g_ref, sh_ref, sc_ref, gm_ref, s0r_ref, s0i_ref, win_ref, bre_ref, bim_ref,
               cre_ref, cim_ref, ar_ref, ai_ref, d_ref, wglu_ref,
               xo_ref, fr_ref, fi_ref,
               xt_ref, u_ref, sre_ref, sim_ref, y_ref, *, nb, lc, batch_major):
    c = pl.program_id(0)
    rows = nb * lc
    d = u_ref.shape[1]
    ns = sre_ref.shape[1]
    nblk = bre_ref.shape[0]
    kb = d // nblk
    sb = ns // nblk

    @pl.when(c == 0)
    def _():
        fr_ref[...] = s0r_ref[...]
        fi_ref[...] = s0i_ref[...]

    if batch_major:
        xt_ref[...] = jnp.swapaxes(x_ref[...], 0, 1).reshape(rows, d)
    else:
        xt_ref[...] = x_ref[...]
    x3 = xt_ref[...].reshape(lc, nb, d)
    h = _norm_mod(x3, g_ref[...], sc_ref[...], sh_ref[...]).reshape(rows, d).astype(BF16)
    u = _dot(h, win_ref[...])
    u_ref[...] = u
    ub = u.astype(BF16)
    for j in range(nblk):
        uj = ub[:, j * kb:(j + 1) * kb]
        sre_ref[:, j * sb:(j + 1) * sb] = _dot(uj, bre_ref[j])
        sim_ref[:, j * sb:(j + 1) * sb] = _dot(uj, bim_ref[j])

    cb = max(LANES, (4 * SUBLANES * LANES) // nb)
    unroll = 8 if lc % 8 == 0 else 1
    for k in range(ns // cb):
        cs = slice(k * cb, (k + 1) * cb)
        a_r = jnp.broadcast_to(ar_ref[:, cs], (nb, cb))
        a_i = jnp.broadcast_to(ai_ref[:, cs], (nb, cb))

        def body(i, carry, cs=cs, a_r=a_r, a_i=a_i):
            sr, si = carry
            for jj in range(unroll):
                r0 = pl.multiple_of((i * unroll + jj) * nb, nb)
                br = sre_ref[pl.ds(r0, nb), cs]
                bi = sim_ref[pl.ds(r0, nb), cs]
                nr = a_r * sr - a_i * si + br
                ni = a_r * si + a_i * sr + bi
                sre_ref[pl.ds(r0, nb), cs] = nr
                sim_ref[pl.ds(r0, nb), cs] = ni
                sr, si = nr, ni
            return sr, si

        sr, si = lax.fori_loop(0, lc // unroll, body, (fr_ref[:, cs], fi_ref[:, cs]))
        fr_ref[:, cs] = sr
        fi_ref[:, cs] = si

    for j in range(nblk):
        sr = sre_ref[:, j * sb:(j + 1) * sb].astype(BF16)
        si = sim_ref[:, j * sb:(j + 1) * sb].astype(BF16)
        yj = _dot(sr, cre_ref[j]) - _dot(si, cim_ref[j])
        y_ref[:, j * kb:(j + 1) * kb] = yj + d_ref[:, j * kb:(j + 1) * kb] * u_ref[:, j * kb:(j + 1) * kb]

    yg = jax.nn.gelu(y_ref[...]).astype(BF16)
    vg = _dot(yg, wglu_ref[...])
    out = vg[:, :d] * jax.nn.sigmoid(vg[:, d:])
    xn = xt_ref[...].reshape(lc, nb, d) + gm_ref[...] * out.reshape(lc, nb, d)
    if batch_major:
        xo_ref[...] = jnp.swapaxes(xn, 0, 1)
    else:
        xo_ref[...] = xn.reshape(rows, d)


def _s5_layer(x, g, sh, sc, gm, s0r, s0i, wts, *, batch_major, lc):
    win, bre, bim, cre, cim, ar, ai, dsk, wglu = wts
    nb = sh.shape[0]
    d = win.shape[0]
    ns = s0r.shape[1]
    if batch_major:
        L = x.shape[1]
        x_spec = pl.BlockSpec((nb, lc, d), lambda c: (0, c, 0))
        x_shape = jax.ShapeDtypeStruct((nb, L, d), F32)
    else:
        L = x.shape[0] // nb
        x_spec = pl.BlockSpec((nb * lc, d), lambda c: (c, 0))
        x_shape = jax.ShapeDtypeStruct((L * nb, d), F32)
    rows = nb * lc
    const = lambda shape: pl.BlockSpec(shape, lambda c: (0,) * len(shape))
    mod3 = lambda t: t.reshape(1, nb, d)
    st_shape = jax.ShapeDtypeStruct((nb, ns), F32)
    kern = functools.partial(_s5_kernel, nb=nb, lc=lc, batch_major=batch_major)
    return pl.pallas_call(
        kern,
        grid=(L // lc,),
        in_specs=[x_spec, const((1, 1, d)), const((1, nb, d)), const((1, nb, d)), const((1, nb, d)),
                  const((nb, ns)), const((nb, ns)), const(win.shape), const(bre.shape), const(bim.shape),
                  const(cre.shape), const(cim.shape), const((1, ns)), const((1, ns)), const((1, d)),
                  const(wglu.shape)],
        out_specs=(x_spec, const((nb, ns)), const((nb, ns))),
        out_shape=(x_shape, st_shape, st_shape),
        scratch_shapes=[pltpu.VMEM((rows, d), F32), pltpu.VMEM((rows, d), F32),
                        pltpu.VMEM((rows, ns), F32), pltpu.VMEM((rows, ns), F32),
                        pltpu.VMEM((rows, d), F32)],
        compiler_params=_cparams(("arbitrary",), 56),
        name="s5",
    )(x, g.reshape(1, 1, d), mod3(sh), mod3(sc), mod3(gm), s0r, s0i, win, bre, bim, cre, cim,
      ar.reshape(1, ns), ai.reshape(1, ns), dsk.reshape(1, d), wglu)


def _router_kernel(x_ref, g_ref, sh_ref, sc_ref, wrt_ref, br_ref, h_ref, ti_ref, tg_ref):
    x = x_ref[0]
    h = _norm_mod(x, g_ref[0], sc_ref[0], sh_ref[0])
    hb = h.astype(BF16)
    h_ref[...] = hb
    h_lo = (h - hb.astype(F32)).astype(BF16)
    lt = (_dot_nt(wrt_ref[0], hb) + _dot_nt(wrt_ref[0], h_lo) + _dot_nt(wrt_ref[1], hb)
          + br_ref[...])
    ne = lt.shape[0]
    idx = lax.broadcasted_iota(jnp.int32, lt.shape, 0)
    vals, ids = [], []
    for _ in range(TOP_K):
        m = jnp.max(lt, axis=0, keepdims=True)
        i = jnp.min(jnp.where(lt == m, idx, ne), axis=0, keepdims=True)
        vals.append(m)
        ids.append(i)
        lt = jnp.where(idx == i, -jnp.inf, lt)
    es = [jnp.exp(v - vals[0]) for v in vals]
    tot = es[0] + es[1] + es[2] + es[3]
    ti_ref[...] = jnp.concatenate(ids, axis=0)
    tg_ref[...] = jnp.concatenate([e / tot for e in es], axis=0)


def _router(x3, g, sh3, sc3, wrt, br, *, tl):
    nbx, L, d = x3.shape
    lm = sh3.shape[1]
    ne = wrt.shape[1]
    nt = L // tl
    t = nbx * L
    mod_spec = (pl.BlockSpec((1, 1, d), lambda b, c: (b, 0, 0)) if lm == 1
                else pl.BlockSpec((1, tl, d), lambda b, c: (b, c, 0)))
    return pl.pallas_call(
        _router_kernel,
        grid=(nbx, nt),
        in_specs=[pl.BlockSpec((1, tl, d), lambda b, c: (b, c, 0)),
                  pl.BlockSpec((1, 1, d), lambda b, c: (0, 0, 0)), mod_spec, mod_spec,
                  pl.BlockSpec((2, ne, d), lambda b, c: (0, 0, 0)), pl.BlockSpec((ne, 1), lambda b, c: (0, 0))],
        out_specs=(pl.BlockSpec((tl, d), lambda b, c: (b * nt + c, 0)),
                   pl.BlockSpec((TOP_K, tl), lambda b, c: (0, b * nt + c)),
                   pl.BlockSpec((TOP_K, tl), lambda b, c: (0, b * nt + c))),
        out_shape=(jax.ShapeDtypeStruct((t, d), BF16), jax.ShapeDtypeStruct((TOP_K, t), jnp.int32),
                   jax.ShapeDtypeStruct((TOP_K, t), F32)),
        compiler_params=_cparams(("arbitrary", "arbitrary"), 32),
        name="router",
    )(x3, g.reshape(1, 1, d), sh3, sc3, wrt, br)


def _gmm_kernel(te_ref, tv_ref, tf_ref, x_ref, wgu_ref, bgu_ref, wd_ref, bd_ref, o_ref,
                wgu_s, wd_s, act_s):
    i = pl.program_id(0)
    de = wd_s.shape[0]
    nch = 4
    cw = de // nch

    @pl.when(tf_ref[i] == 1)
    def _():
        for n in range(2 * nch):
            wgu_s[:, n * cw:(n + 1) * cw] = wgu_ref[0, :, n * cw:(n + 1) * cw].astype(BF16)
        for n in range(nch):
            wd_s[n * cw:(n + 1) * cw, :] = wd_ref[0, n * cw:(n + 1) * cw, :].astype(BF16)

    @pl.when(tv_ref[i] == 1)
    def _():
        x = x_ref[...]
        for n in range(nch):
            glu = _dot(x, wgu_s[:, n * cw:(n + 1) * cw]) + bgu_ref[0, :, n * cw:(n + 1) * cw]
            lin = _dot(x, wgu_s[:, de + n * cw:de + (n + 1) * cw]) + bgu_ref[0, :, de + n * cw:de + (n + 1) * cw]
            glu = jnp.minimum(glu, SWIGLU_LIMIT)
            lin = jnp.clip(lin, -SWIGLU_LIMIT, SWIGLU_LIMIT)
            act = glu * jax.nn.sigmoid(SWIGLU_ALPHA * glu) * (lin + 1.0)
            act_s[:, n * cw:(n + 1) * cw] = act.astype(BF16)
        o_ref[...] = _dot(act_s[...], wd_s[...]) + bd_ref[0]

    @pl.when(tv_ref[i] == 0)
    def _():
        o_ref[...] = jnp.zeros_like(o_ref)


def _gmm(xs, tile_expert, tile_valid, tile_first, w_gu, b_gu, w_down, b_down, *, tm):
    n_slots, d = xs.shape
    ne, _, de2 = w_gu.shape
    de = de2 // 2
    n_tiles = n_slots // tm
    grid_spec = pltpu.PrefetchScalarGridSpec(
        num_scalar_prefetch=3,
        grid=(n_tiles,),
        in_specs=[pl.BlockSpec((tm, d), lambda i, te, tv, tf: (i, 0)),
                  pl.BlockSpec((1, d, de2), lambda i, te, tv, tf: (te[i], 0, 0)),
                  pl.BlockSpec((1, 1, de2), lambda i, te, tv, tf: (te[i], 0, 0)),
                  pl.BlockSpec((1, de, d), lambda i, te, tv, tf: (te[i], 0, 0)),
                  pl.BlockSpec((1, 1, d), lambda i, te, tv, tf: (te[i], 0, 0))],
        out_specs=pl.BlockSpec((tm, d), lambda i, te, tv, tf: (i, 0)),
        scratch_shapes=[pltpu.VMEM((d, de2), BF16), pltpu.VMEM((de, d), BF16), pltpu.VMEM((tm, de), BF16)],
    )
    return pl.pallas_call(
        _gmm_kernel,
        grid_spec=grid_spec,
        out_shape=jax.ShapeDtypeStruct((n_slots, d), F32),
        compiler_params=_cparams(("arbitrary",), 56),
        name="gmm",
    )(tile_expert, tile_valid, tile_first, xs, w_gu, b_gu.reshape(ne, 1, de2), w_down, b_down.reshape(ne, 1, d))


def _moe(h_all, top_i, gates, w_gu, b_gu, w_down, b_down):
    t, d = h_all.shape
    ne = w_gu.shape[0]
    tm = GMM_ROWS
    a = TOP_K * t
    e_flat = top_i.reshape(-1)
    tok_flat = jnp.tile(jnp.arange(t, dtype=jnp.int32), TOP_K)
    counts = jnp.zeros((ne,), jnp.int32).at[e_flat].add(1)
    padded = (counts + tm - 1) // tm * tm
    pad_end = jnp.cumsum(padded)
    pad_start = pad_end - padded
    cnt_start = jnp.cumsum(counts) - counts
    order = jnp.argsort(e_flat, stable=True)
    e_sorted = e_flat[order]
    dest = pad_start[e_sorted] + (jnp.arange(a, dtype=jnp.int32) - cnt_start[e_sorted])
    n_tiles = -(-a // tm) + ne
    n_slots = n_tiles * tm
    slot_tok = jnp.zeros((n_slots,), jnp.int32).at[dest].set(tok_flat[order])
    pos = jnp.zeros((a,), jnp.int32).at[order].set(dest)
    tile_start = jnp.arange(n_tiles, dtype=jnp.int32) * tm
    tile_expert = jnp.minimum(jnp.searchsorted(pad_end, tile_start, side='right'), ne - 1).astype(jnp.int32)
    tile_valid = (tile_start < pad_end[-1]).astype(jnp.int32)
    prev = jnp.concatenate([jnp.full((1,), -1, jnp.int32), tile_expert[:-1]])
    tile_first = (tile_expert != prev).astype(jnp.int32)
    xs = jnp.take(h_all, slot_tok, axis=0)
    outs = _gmm(xs, tile_expert, tile_valid, tile_first, w_gu, b_gu, w_down, b_down, tm=tm)
    pos_k = pos.reshape(TOP_K, t)
    y = jnp.zeros((t, d), F32)
    for k in range(TOP_K):
        y = y + gates[k][:, None] * jnp.take(outs, pos_k[k], axis=0)
    return y


def _mlstm_kernel(x_ref, g_ref, sh_ref, sc_ref, gm_ref, wup_ref, cw_ref, cb_ref, wq_ref, wk_ref, wv_ref,
                  wg_ref, bg_ref, lnw_ref, skip_ref, wdn_ref,
                  xo_ref, cst_ref, nst_ref, mst_ref, conv_ref,
                  xm_s, up_s, xc_s, q_s, k_s, v_s, *, lc, heads, dh):
    c = pl.program_id(1)
    inner = heads * dh
    pad = SUBLANES

    @pl.when(c == 0)
    def _():
        cst_ref[...] = jnp.zeros_like(cst_ref)
        nst_ref[...] = jnp.zeros_like(nst_ref)
        mst_ref[...] = jnp.zeros_like(mst_ref)
        xm_s[0:pad, :] = jnp.zeros((pad, inner), F32)

    x = x_ref[0]
    h = _norm_mod(x, g_ref[0], sc_ref[0], sh_ref[0]).astype(BF16)
    up_s[...] = _dot(h, wup_ref[...])
    xm_s[pad:pad + lc, :] = up_s[:, :inner]
    conv = cb_ref[...]
    for k in range(ML_CONV):
        off = pad - (ML_CONV - 1) + k
        conv = conv + cw_ref[k:k + 1, :] * xm_s[off:off + lc, :]
    xc_s[...] = _silu(conv)
    xm_s[0:pad, :] = xm_s[lc:lc + pad, :]
    conv_ref[0] = xm_s[0:pad, :]

    gacc = jnp.broadcast_to(bg_ref[...], (lc, LANES))
    for hd in range(heads):
        hs = slice(hd * dh, (hd + 1) * dh)
        xch = xc_s[:, hs].astype(BF16)
        q = _dot(xch, wq_ref[hd])
        k = _dot(xch, wk_ref[hd]) * (dh ** -0.5)
        v = _dot(up_s[:, hs].astype(BF16), wv_ref[hd])
        qb = q.astype(BF16)
        vb = v.astype(BF16)
        q_s[:, hs] = qb
        k_s[:, hs] = k
        v_s[:, hs] = vb
        gacc = gacc + _dot(qb, wg_ref[0, hd]) + _dot(k.astype(BF16), wg_ref[1, hd]) + _dot(vb, wg_ref[2, hd])

    lf = _log_sigmoid(gacc)
    row = lax.broadcasted_iota(jnp.int32, (lc, lc), 0)
    col = lax.broadcasted_iota(jnp.int32, (lc, lc), 1)
    causal = row >= col
    tri = jnp.where(causal, 1.0, 0.0).astype(BF16)
    hi = lf.astype(BF16)
    r1 = lf - hi.astype(F32)
    mid = r1.astype(BF16)
    lo = (r1 - mid.astype(F32)).astype(BF16)
    bcum = _dot(tri, hi) + _dot(tri, mid) + _dot(tri, lo)
    g_t = gacc.T
    b_t = bcum.T

    acc = jnp.zeros((lc, xo_ref.shape[2]), F32)
    for hd in range(heads):
        hs = slice(hd * dh, (hd + 1) * dh)
        ig_c = gacc[:, hd:hd + 1]
        b_c = bcum[:, heads + hd:heads + hd + 1]
        ig_r = g_t[hd:hd + 1, :]
        b_r = b_t[heads + hd:heads + hd + 1, :]
        m_prev = mst_ref[0, hd][:, 0:1]
        dm = jnp.where(causal, (b_c - b_r) + ig_r, -jnp.inf)
        inter = b_c + m_prev
        m_t = jnp.maximum(inter, jnp.max(dm, axis=1, keepdims=True))
        w_intra = jnp.exp(dm - m_t)
        w_inter = jnp.exp(inter - m_t)
        qb = q_s[:, hs]
        kf = k_s[:, hs]
        vb = v_s[:, hs]
        cmat = cst_ref[0, hd]
        nvec = nst_ref[0, hd]
        qk = _dot_nt(qb, kf.astype(BF16)) * w_intra
        num = _dot(qk.astype(BF16), vb) + w_inter * _dot(qb, cmat.astype(BF16))
        qn = jnp.sum(qb.astype(F32) * nvec.astype(BF16).astype(F32), axis=1, keepdims=True)
        den = jnp.sum(qk, axis=1, keepdims=True) + w_inter * qn
        hh = num / jnp.maximum(jnp.abs(den), jnp.exp(-m_t))

        bl = b_c[lc - 1:lc, :]
        m_new = m_t[lc - 1:lc, :]
        w_s = jnp.exp(bl - b_c + ig_c - m_new)
        decay = jnp.exp(bl + m_prev - m_new)
        kw = w_s * kf
        cst_ref[0, hd] = decay * cmat + _dot(kw.T.astype(BF16), vb)
        nst_ref[0, hd] = decay * nvec + jnp.sum(kw, axis=0, keepdims=True)
        mst_ref[0, hd] = jnp.broadcast_to(m_new, (1, LANES))

        mu = jnp.mean(hh, axis=1, keepdims=True)
        hc = hh - mu
        var = jnp.mean(hc * hc, axis=1, keepdims=True)
        hn = hc * lax.rsqrt(var + EPS) * lnw_ref[:, hs]
        o = (hn + skip_ref[:, hs] * xc_s[:, hs]) * _silu(up_s[:, inner + hd * dh:inner + (hd + 1) * dh])
        acc = acc + _dot(o.astype(BF16), wdn_ref[hs, :])
    xo_ref[0] = x + gm_ref[0] * acc


def _mlstm_prompt(x, g, sh, sc, gm, wts, *, lc):
    wup, cw, cb, wq, wk, wv, wg, bg, lnw, skip, wdn = wts
    nb, L, d = x.shape
    heads, dh, _ = wq.shape
    inner = heads * dh
    const = lambda shape: pl.BlockSpec(shape, lambda b, c: (0,) * len(shape), pipeline_mode=pl.Buffered(1))
    per_b = lambda shape: pl.BlockSpec(shape, lambda b, c: (b,) + (0,) * (len(shape) - 1))
    mod3 = lambda t: t.reshape(nb, 1, d)
    kern = functools.partial(_mlstm_kernel, lc=lc, heads=heads, dh=dh)
    return pl.pallas_call(
        kern,
        grid=(nb, L // lc),
        in_specs=[pl.BlockSpec((1, lc, d), lambda b, c: (b, c, 0)), const((1, 1, d)),
                  per_b((1, 1, d)), per_b((1, 1, d)), per_b((1, 1, d)),
                  const(wup.shape), const(cw.shape), const((1, inner)), const(wq.shape), const(wk.shape),
                  const(wv.shape), const(wg.shape), const((1, LANES)), const((1, inner)), const((1, inner)),
                  const(wdn.shape)],
        out_specs=(pl.BlockSpec((1, lc, d), lambda b, c: (b, c, 0)),
                   per_b((1, heads, dh, dh)), per_b((1, heads, 1, dh)), per_b((1, heads, 1, LANES)),
                   per_b((1, SUBLANES, inner))),
        out_shape=(jax.ShapeDtypeStruct((nb, L, d), F32),
                   jax.ShapeDtypeStruct((nb, heads, dh, dh), F32),
                   jax.ShapeDtypeStruct((nb, heads, 1, dh), F32),
                   jax.ShapeDtypeStruct((nb, heads, 1, LANES), F32),
                   jax.ShapeDtypeStruct((nb, SUBLANES, inner), F32)),
        scratch_shapes=[pltpu.VMEM((lc + SUBLANES, inner), F32), pltpu.VMEM((lc, 2 * inner), F32),
                        pltpu.VMEM((lc, inner), F32), pltpu.VMEM((lc, inner), BF16),
                        pltpu.VMEM((lc, inner), F32), pltpu.VMEM((lc, inner), BF16)],
        compiler_params=_cparams(("arbitrary", "arbitrary"), 58),
        name="mlstm",
    )(x, g.reshape(1, 1, d), mod3(sh), mod3(sc), mod3(gm), wup, cw, cb.reshape(1, inner), wq, wk, wv, wg, bg,
      lnw.reshape(1, inner), skip.reshape(1, inner), wdn)


def _mls_proj_kernel(x_ref, g_ref, sh_ref, sc_ref, wxm_ref, wz_ref, buf_ref, cw_ref, cb_ref, wq_ref, wk_ref,
                     wv_ref, wg_ref, bg_ref,
                     q_ref, k_ref, v_ref, xc_ref, z_ref, gate_ref, nbuf_ref):
    hd = pl.program_id(0)

    @pl.when(hd == 0)
    def _():
        gate_ref[...] = jnp.broadcast_to(bg_ref[...], gate_ref.shape)

    h = _norm_mod(x_ref[...], g_ref[...], sc_ref[...], sh_ref[...]).astype(BF16)
    xm = _dot(h, wxm_ref[...])
    z_ref[...] = _dot(h, wz_ref[...])
    conv = cb_ref[...] + cw_ref[ML_CONV - 1:ML_CONV, :] * xm
    for k in range(ML_CONV - 1):
        conv = conv + cw_ref[k:k + 1, :] * buf_ref[k]
    for k in range(ML_CONV - 2):
        nbuf_ref[k] = buf_ref[k + 1]
    nbuf_ref[ML_CONV - 2] = xm
    xc = _silu(conv)
    xc_ref[...] = xc
    xcb = xc.astype(BF16)
    dh = xm.shape[1]
    q = _dot(xcb, wq_ref[0])
    k = _dot(xcb, wk_ref[0]) * (dh ** -0.5)
    v = _dot(xm.astype(BF16), wv_ref[0])
    q_ref[...] = q
    k_ref[...] = k
    v_ref[...] = v
    gate_ref[...] += (_dot(q.astype(BF16), wg_ref[0, 0]) + _dot(k.astype(BF16), wg_ref[1, 0])
                      + _dot(v.astype(BF16), wg_ref[2, 0]))


def _mls_proj(x, g, sh, sc, buf, wts):
    wup, cw, cb, wq, wk, wv, wg, bg = wts
    nb, d = x.shape
    heads, dh, _ = wq.shape
    inner = heads * dh
    full = lambda shape: pl.BlockSpec(shape, lambda h: (0,) * len(shape))
    colblk = pl.BlockSpec((nb, dh), lambda h: (0, h))
    act = jax.ShapeDtypeStruct((nb, inner), F32)
    return pl.pallas_call(
        _mls_proj_kernel,
        grid=(heads,),
        in_specs=[full((nb, d)), full((1, d)), full((nb, d)), full((nb, d)),
                  pl.BlockSpec((d, dh), lambda h: (0, h)), pl.BlockSpec((d, dh), lambda h: (0, heads + h)),
                  pl.BlockSpec((ML_CONV - 1, nb, dh), lambda h: (0, 0, h)),
                  pl.BlockSpec((ML_CONV, dh), lambda h: (0, h)), pl.BlockSpec((1, dh), lambda h: (0, h)),
                  pl.BlockSpec((1, dh, dh), lambda h: (h, 0, 0)), pl.BlockSpec((1, dh, dh), lambda h: (h, 0, 0)),
                  pl.BlockSpec((1, dh, dh), lambda h: (h, 0, 0)),
                  pl.BlockSpec((3, 1, dh, LANES), lambda h: (0, h, 0, 0)), full((1, LANES))],
        out_specs=(colblk, colblk, colblk, colblk, colblk, full((nb, LANES)),
                   pl.BlockSpec((ML_CONV - 1, nb, dh), lambda h: (0, 0, h))),
        out_shape=(act, act, act, act, act, jax.ShapeDtypeStruct((nb, LANES), F32),
                   jax.ShapeDtypeStruct((ML_CONV - 1, nb, inner), F32)),
        compiler_params=_cparams(("arbitrary",), 32),
        name="mls_proj",
    )(x, g.reshape(1, d), sh, sc, wup, wup, buf, cw, cb.reshape(1, inner), wq, wk, wv, wg, bg)


def _mls_gate_scalars(ig, fg, m0):
    lf = _log_sigmoid(fg)
    inter = lf + m0
    m_t = jnp.maximum(inter, ig)
    return m_t, jnp.exp(ig - m_t), jnp.exp(inter - m_t)


def _mls_state_kernel(q_ref, k_ref, v_ref, ig_ref, fg_ref, m0_ref, c_ref, qc_ref, cn_ref, *, heads, dh):
    _, w_in, w_dec = _mls_gate_scalars(ig_ref[0], fg_ref[0], m0_ref[0])
    rowmask = lax.broadcasted_iota(jnp.int32, (LANES, dh), 0) == 0
    for hd in range(heads):
        hs = slice(hd * dh, (hd + 1) * dh)
        wi = w_in[:, hd:hd + 1]
        wd = w_dec[:, hd:hd + 1]
        cmat = c_ref[0, hd]
        q8 = jnp.broadcast_to(q_ref[0, :, hs], (SUBLANES, dh)).astype(BF16)
        qc_ref[0, :, hs] = _dot(q8, cmat.astype(BF16))[0:1, :]
        kw = jnp.where(rowmask, wi * k_ref[0, :, hs], 0.0)
        vv = jnp.where(rowmask, v_ref[0, :, hs], 0.0)
        cn_ref[0, hd] = wd * cmat + _dot(kw.T.astype(BF16), vv.astype(BF16))


def _mls_state(q, k, v, ig, fg, m0, cst):
    nb, inner = q.shape
    _, heads, dh, _ = cst.shape
    row = lambda w: pl.BlockSpec((1, 1, w), lambda b: (b, 0, 0))
    r3 = lambda t: t.reshape(nb, 1, t.shape[1])
    kern = functools.partial(_mls_state_kernel, heads=heads, dh=dh)
    qc, cn = pl.pallas_call(
        kern,
        grid=(nb,),
        in_specs=[row(inner), row(inner), row(inner), row(LANES), row(LANES), row(LANES),
                  pl.BlockSpec((1, heads, dh, dh), lambda b: (b, 0, 0, 0))],
        out_specs=(row(inner), pl.BlockSpec((1, heads, dh, dh), lambda b: (b, 0, 0, 0))),
        out_shape=(jax.ShapeDtypeStruct((nb, 1, inner), F32), jax.ShapeDtypeStruct(cst.shape, F32)),
        compiler_params=_cparams(("arbitrary",), 40),
        name="mls_state",
    )(r3(q), r3(k), r3(v), r3(ig), r3(fg), r3(m0), cst)
    return qc.reshape(nb, inner), cn


def _mls_post_kernel(x_ref, gm_ref, q_ref, k_ref, v_ref, xc_ref, z_ref, qc_ref, n_ref, ig_ref, fg_ref, m0_ref,
                     lnw_ref, skip_ref, wdn_ref, xo_ref, nn_ref, mn_ref, *, heads, dh):
    m_t, w_in, w_dec = _mls_gate_scalars(ig_ref[...], fg_ref[...], m0_ref[...])
    mn_ref[...] = m_t
    acc = jnp.zeros(xo_ref.shape, F32)
    for hd in range(heads):
        hs = slice(hd * dh, (hd + 1) * dh)
        wi = w_in[:, hd:hd + 1]
        wd = w_dec[:, hd:hd + 1]
        mt = m_t[:, hd:hd + 1]
        q = q_ref[:, hs]
        k = k_ref[:, hs]
        qr = q.astype(BF16).astype(F32)
        qk = jnp.sum(qr * k.astype(BF16).astype(F32), axis=1, keepdims=True) * wi
        nvec = n_ref[:, hs]
        num = qk.astype(BF16).astype(F32) * v_ref[:, hs].astype(BF16).astype(F32) + wd * qc_ref[:, hs]
        den = qk + wd * jnp.sum(qr * nvec.astype(BF16).astype(F32), axis=1, keepdims=True)
        hh = num / jnp.maximum(jnp.abs(den), jnp.exp(-mt))
        nn_ref[:, hs] = wd * nvec + wi * k
        mu = jnp.mean(hh, axis=1, keepdims=True)
        hc = hh - mu
        var = jnp.mean(hc * hc, axis=1, keepdims=True)
        hn = hc * lax.rsqrt(var + EPS) * lnw_ref[:, hs]
        o = (hn + skip_ref[:, hs] * xc_ref[:, hs]) * _silu(z_ref[:, hs])
        acc = acc + _dot(o.astype(BF16), wdn_ref[hs, :])
    xo_ref[...] = x_ref[...] + gm_ref[...] * acc


def _mls_post(x, gm, q, k, v, xc, z, qc, n0, ig, fg, m0, lnw, skip, wdn, *, heads, dh):
    nb, d = x.shape
    inner = heads * dh
    kern = functools.partial(_mls_post_kernel, heads=heads, dh=dh)
    return pl.pallas_call(
        kern,
        out_shape=(jax.ShapeDtypeStruct((nb, d), F32), jax.ShapeDtypeStruct((nb, inner), F32),
                   jax.ShapeDtypeStruct((nb, LANES), F32)),
        compiler_params=pltpu.CompilerParams(vmem_limit_bytes=40 * MIB),
        name="mls_post",
    )(x, gm, q, k, v, xc, z, qc, n0, ig, fg, m0, lnw.reshape(1, inner), skip.reshape(1, inner), wdn)


def _final_kernel(x_ref, g_ref, o_ref):
    x = x_ref[...]
    o_ref[...] = x * lax.rsqrt(jnp.mean(x * x, axis=-1, keepdims=True) + EPS) * g_ref[...]


def _final_norm(x, g):
    t, d = x.shape
    tm = min(NORM_ROWS, t)
    return pl.pallas_call(
        _final_kernel,
        grid=(t // tm,),
        in_specs=[pl.BlockSpec((tm, d), lambda i: (i, 0)), pl.BlockSpec((1, d), lambda i: (0, 0))],
        out_specs=pl.BlockSpec((tm, d), lambda i: (i, 0)),
        out_shape=jax.ShapeDtypeStruct((t, d), F32),
        compiler_params=_cparams(("arbitrary",), 32),
        name="final_norm",
    )(x, g.reshape(1, d))


def _pad_lanes(t):
    return jnp.pad(t, ((0, 0), (0, LANES - t.shape[1])))


def kernel(x_prompt, x_sample, c_prompt, c_sample, state_s5_re, state_s5_im, state_mlstm_C, state_mlstm_n, state_mlstm_m, state_mlstm_conv, norm_mix_g, norm_ffn_g, final_norm_g, ada_w, ada_b, s5_w_in, s5_lam_re, s5_lam_im, s5_log_dt, s5_b_re, s5_b_im, s5_c_re, s5_c_im, s5_d, s5_w_glu, ml_w_up, ml_conv_w, ml_conv_b, ml_w_q, ml_w_k, ml_w_v, ml_w_gate, ml_b_gate, ml_ln_w, ml_skip, ml_w_down, moe_w_router, moe_b_router, moe_w_gu, moe_b_gu, moe_w_down, moe_b_down):
    bp, seq, d = x_prompt.shape
    bs = x_sample.shape[0]
    tp = bp * seq
    _, groups, pstate, gch = s5_b_re.shape
    ns = groups * pstate
    heads, dh = ml_w_q.shape[1], ml_w_q.shape[2]
    inner = heads * dh
    ne = moe_w_router.shape[2]

    mods = _ada(jnp.concatenate([c_prompt, c_sample], axis=0), ada_w, ada_b)

    def mod(i, j):
        m = mods[i, :, j * d:(j + 1) * d]
        return m[:bp], m[bp:]

    ar, ai, bbr, bbi = _s5_prep(s5_lam_re[0], s5_lam_im[0], s5_log_dt[0], s5_b_re[0], s5_b_im[0])
    nblk = d // LANES
    bre = _block_diag(bbr.reshape(groups, gch, pstate), nblk).astype(BF16)
    bim = _block_diag(bbi.reshape(groups, gch, pstate), nblk).astype(BF16)
    cre = _block_diag(s5_c_re[0].transpose(0, 2, 1), nblk).astype(BF16)
    cim = _block_diag(s5_c_im[0].transpose(0, 2, 1), nblk).astype(BF16)
    s5_wts = (s5_w_in[0].astype(BF16), bre, bim, cre, cim, ar.reshape(ns), ai.reshape(ns), s5_d[0],
              s5_w_glu[0].astype(BF16))
    (sh_p, sh_s), (sc_p, sc_s), (gm_p, gm_s) = mod(0, 0), mod(0, 1), mod(0, 2)
    zeros_st = jnp.zeros((bp, ns), F32)
    xp, p_re, p_im = _s5_layer(x_prompt, norm_mix_g[0], sh_p, sc_p, gm_p, zeros_st, zeros_st, s5_wts,
                               batch_major=True, lc=S5_CHUNK)
    xs_, s_re, s_im = _s5_layer(x_sample.reshape(bs, d), norm_mix_g[0], sh_s, sc_s, gm_s,
                                state_s5_re[0].reshape(bs, ns), state_s5_im[0].reshape(bs, ns), s5_wts,
                                batch_major=False, lc=1)

    def moe_layer(i, xp, xs_):
        (sh_p, sh_s), (sc_p, sc_s), (gf_p, gf_s) = mod(i, 3), mod(i, 4), mod(i, 5)
        wr_t = moe_w_router[i].T
        wr_hi = wr_t.astype(BF16)
        wrt = jnp.stack([wr_hi, (wr_t - wr_hi.astype(F32)).astype(BF16)])
        br = moe_b_router[i].reshape(ne, 1)
        hp, tip, tgp = _router(xp, norm_ffn_g[i], sh_p[:, None], sc_p[:, None], wrt, br, tl=ROUTER_ROWS)
        hs, tis, tgs = _router(xs_[None], norm_ffn_g[i], sh_s[None], sc_s[None], wrt, br, tl=bs)
        y = _moe(jnp.concatenate([hp, hs], axis=0), jnp.concatenate([tip, tis], axis=1),
                 jnp.concatenate([tgp, tgs], axis=1), moe_w_gu[i], moe_b_gu[i], moe_w_down[i], moe_b_down[i])
        xp = xp + gf_p[:, None] * y[:tp].reshape(bp, seq, d)
        xs_ = xs_ + gf_s * y[tp:]
        return xp, xs_

    xp, xs_ = moe_layer(0, xp, xs_)

    wg = _pad_lanes(ml_w_gate[0]).reshape(3, heads, dh, LANES).astype(BF16)
    bg = _pad_lanes(ml_b_gate[0][None])
    wup = ml_w_up[0].astype(BF16)
    wq, wk, wv = ml_w_q[0].astype(BF16), ml_w_k[0].astype(BF16), ml_w_v[0].astype(BF16)
    wdn = ml_w_down[0].astype(BF16)
    (sh_p, sh_s), (sc_p, sc_s), (gm_p, gm_s) = mod(1, 0), mod(1, 1), mod(1, 2)
    xp, p_c, p_n, p_m, p_tail = _mlstm_prompt(
        xp, norm_mix_g[1], sh_p, sc_p, gm_p,
        (wup, ml_conv_w[0], ml_conv_b[0], wq, wk, wv, wg, bg, ml_ln_w[0], ml_skip[0], wdn), lc=ML_CHUNK)
    p_n = p_n.reshape(bp, heads, dh)
    p_m = p_m[:, :, 0, 0]
    p_conv = p_tail[:, SUBLANES - (ML_CONV - 1):, :]

    buf = state_mlstm_conv[0].transpose(1, 0, 2)
    q, k, v, xc, z, gates, nbuf = _mls_proj(xs_, norm_mix_g[1], sh_s, sc_s, buf,
                                            (wup, ml_conv_w[0], ml_conv_b[0], wq, wk, wv, wg, bg))
    ig = _pad_lanes(gates[:, :heads])
    fg = _pad_lanes(gates[:, heads:2 * heads])
    m0 = _pad_lanes(state_mlstm_m[0])
    qc, s_c = _mls_state(q, k, v, ig, fg, m0, state_mlstm_C[0])
    xs_, s_n, s_m = _mls_post(xs_, gm_s, q, k, v, xc, z, qc, state_mlstm_n[0].reshape(bs, inner), ig, fg, m0,
                              ml_ln_w[0], ml_skip[0], wdn, heads=heads, dh=dh)
    s_n = s_n.reshape(bs, heads, dh)
    s_m = s_m[:, :heads]
    s_conv = nbuf.transpose(1, 0, 2)

    xp, xs_ = moe_layer(1, xp, xs_)

    y_p = _final_norm(xp.reshape(tp, d), final_norm_g).reshape(bp, seq, d)
    y_s = _final_norm(xs_, final_norm_g).reshape(bs, 1, d)
    return (y_p, y_s,
            p_re.reshape(1, bp, groups, pstate), p_im.reshape(1, bp, groups, pstate),
            p_c[None], p_n[None], p_m[None], p_conv[None],
            s_re.reshape(1, bs, groups, pstate), s_im.reshape(1, bs, groups, pstate),
            s_c[None], s_n[None], s_m[None], s_conv[None])
```

```python
import functools

import jax
import jax.numpy as jnp
from jax import lax
from jax.experimental import pallas as pl
from jax.experimental.pallas import tpu as pltpu

F32 = jnp.float32
BF16 = jnp.bfloat16
EPS = 1e-6
TOP_K = 4
SWIGLU_LIMIT = 7.0
SWIGLU_ALPHA = 1.702
ML_CONV = 4

LANES = 128
SUBLANES = 8
MIB = 1024 * 1024

S5_CHUNK = 64
ML_CHUNK = 256
ROUTER_ROWS = 512
GMM_ROWS = 512
NORM_ROWS = 512


def _cparams(semantics, vmem_mib):
    return pltpu.CompilerParams(dimension_semantics=semantics, vmem_limit_bytes=int(vmem_mib * MIB))


def _dot(a, b):
    return jnp.dot(a, b, preferred_element_type=F32)


def _dot_nt(a, b):
    return lax.dot_general(a, b, (((1,), (1,)), ((), ())), preferred_element_type=F32)


def _norm_mod(x, g, sc, sh):
    y = x * lax.rsqrt(jnp.mean(x * x, axis=-1, keepdims=True) + EPS)
    return (y * g) * (1.0 + sc) + sh


def _silu(x):
    return x * jax.nn.sigmoid(x)


def _log_sigmoid(x):
    return -(jnp.maximum(-x, 0.0) + jnp.log1p(jnp.exp(-jnp.abs(x))))


def _ada_kernel(c_ref, w_ref, b_ref, o_ref):
    c = c_ref[...]
    o_ref[0] = _dot(_silu(c).astype(BF16), w_ref[0].astype(BF16)) + b_ref[0]


def _ada(c_all, ada_w, ada_b):
    depth, d, n = ada_w.shape
    rows = c_all.shape[0]
    tn = n // 4
    return pl.pallas_call(
        _ada_kernel,
        grid=(depth, n // tn),
        in_specs=[pl.BlockSpec((rows, d), lambda i, j: (0, 0)),
                  pl.BlockSpec((1, d, tn), lambda i, j: (i, 0, j)),
                  pl.BlockSpec((1, 1, tn), lambda i, j: (i, 0, j))],
        out_specs=pl.BlockSpec((1, rows, tn), lambda i, j: (i, 0, j)),
        out_shape=jax.ShapeDtypeStruct((depth, rows, n), F32),
        compiler_params=_cparams(("arbitrary", "arbitrary"), 32),
        name="ada",
    )(c_all, ada_w, ada_b.reshape(depth, 1, n))


def _s5_prep_kernel(lr_ref, li_ref, ldt_ref, br_ref, bi_ref, ar_ref, ai_ref, bbr_ref, bbi_ref):
    dt = jnp.exp(ldt_ref[...])
    lr = lr_ref[...]
    li = li_ref[...]
    mag = jnp.exp(lr * dt)
    ar = mag * jnp.cos(li * dt)
    ai = mag * jnp.sin(li * dt)
    den = lr * lr + li * li
    nr = ar - 1.0
    wr = (nr * lr + ai * li) / den
    wi = (ai * lr - nr * li) / den
    br = br_ref[...]
    bi = bi_ref[...]
    ar_ref[...] = ar
    ai_ref[...] = ai
    bbr_ref[...] = wr * br - wi * bi
    bbi_ref[...] = wr * bi + wi * br


def _s5_prep(lam_re, lam_im, log_dt, b_re, b_im):
    g, p, c = b_re.shape
    rep = lambda t: jnp.repeat(t, c, axis=0)
    bt = lambda t: t.transpose(0, 2, 1).reshape(g * c, p)
    shp = jax.ShapeDtypeStruct((g * c, p), F32)
    ar, ai, bbr, bbi = pl.pallas_call(
        _s5_prep_kernel, out_shape=(shp, shp, shp, shp), name="s5_prep",
    )(rep(lam_re), rep(lam_im), rep(log_dt[:, None]), bt(b_re), bt(b_im))
    return ar[::c], ai[::c], bbr, bbi


def _block_diag(t, nblk):
    g, a, b = t.shape
    gl = g // nblk
    t4 = t.reshape(nblk, gl, a, b)
    eye = jnp.eye(gl, dtype=t.dtype)
    return jnp.einsum('jgab,gh->jgahb', t4, eye).reshape(nblk, gl * a, gl * b)


def _s5_kernel(x_ref, g_ref, sh_ref, sc_ref, gm_ref, s0r_ref, s0i_ref, win_ref, bre_ref, bim_ref,
               cre_ref, cim_ref, ar_ref, ai_ref, d_ref, wglu_ref,
               xo_ref, fr_ref, fi_ref,
               xt_ref, u_ref, sre_ref, sim_ref, y_ref, *, nb, lc, batch_major):
    c = pl.program_id(0)
    rows = nb * lc
    d = u_ref.shape[1]
    ns = sre_ref.shape[1]
    nblk = bre_ref.shape[0]
    kb = d // nblk
    sb = ns // nblk

    @pl.when(c == 0)
    def _():
        fr_ref[...] = s0r_ref[...]
        fi_ref[...] = s0i_ref[...]

    if batch_major:
        xt_ref[...] = jnp.swapaxes(x_ref[...], 0, 1).reshape(rows, d)
    else:
        xt_ref[...] = x_ref[...]
    x3 = xt_ref[...].reshape(lc, nb, d)
    h = _norm_mod(x3, g_ref[...], sc_ref[...], sh_ref[...]).reshape(rows, d).astype(BF16)
    u = _dot(h, win_ref[...])
    u_ref[...] = u
    ub = u.astype(BF16)
    for j in range(nblk):
        uj = ub[:, j * kb:(j + 1) * kb]
        sre_ref[:, j * sb:(j + 1) * sb] = _dot(uj, bre_ref[j])
        sim_ref[:, j * sb:(j + 1) * sb] = _dot(uj, bim_ref[j])

    cb = max(LANES, (4 * SUBLANES * LANES) // nb)
    unroll = 8 if lc % 8 == 0 else 1
    for k in range(ns // cb):
        cs = slice(k * cb, (k + 1) * cb)
        a_r = jnp.broadcast_to(ar_ref[:, cs], (nb, cb))
        a_i = jnp.broadcast_to(ai_ref[:, cs], (nb, cb))

        def body(i, carry, cs=cs, a_r=a_r, a_i=a_i):
            sr, si = carry
            for jj in range(unroll):
                r0 = pl.multiple_of((i * unroll + jj) * nb, nb)
                br = sre_ref[pl.ds(r0, nb), cs]
                bi = sim_ref[pl.ds(r0, nb), cs]
                nr = a_r * sr - a_i * si + br
                ni = a_r * si + a_i * sr + bi
                sre_ref[pl.ds(r0, nb), cs] = nr
                sim_ref[pl.ds(r0, nb), cs] = ni
                sr, si = nr, ni
            return sr, si

        sr, si = lax.fori_loop(0, lc // unroll, body, (fr_ref[:, cs], fi_ref[:, cs]))
        fr_ref[:, cs] = sr
        fi_ref[:, cs] = si

    for j in range(nblk):
        sr = sre_ref[:, j * sb:(j + 1) * sb].astype(BF16)
        si = sim_ref[:, j * sb:(j + 1) * sb].astype(BF16)
        yj = _dot(sr, cre_ref[j]) - _dot(si, cim_ref[j])
        y_ref[:, j * kb:(j + 1) * kb] = yj + d_ref[:, j * kb:(j + 1) * kb] * u_ref[:, j * kb:(j + 1) * kb]

    yg = jax.nn.gelu(y_ref[...]).astype(BF16)
    vg = _dot(yg, wglu_ref[...])
    out = vg[:, :d] * jax.nn.sigmoid(vg[:, d:])
    xn = xt_ref[...].reshape(lc, nb, d) + gm_ref[...] * out.reshape(lc, nb, d)
    if batch_major:
        xo_ref[...] = jnp.swapaxes(xn, 0, 1)
    else:
        xo_ref[...] = xn.reshape(rows, d)


def _s5_layer(x, g, sh, sc, gm, s0r, s0i, wts, *, batch_major, lc):
    win, bre, bim, cre, cim, ar, ai, dsk, wglu = wts
    nb = sh.shape[0]
    d = win.shape[0]
    ns = s0r.shape[1]
    if batch_major:
        L = x.shape[1]
        x_spec = pl.BlockSpec((nb, lc, d), lambda c: (0, c, 0))
        x_shape = jax.ShapeDtypeStruct((nb, L, d), F32)
    else:
        L = x.shape[0] // nb
        x_spec = pl.BlockSpec((nb * lc, d), lambda c: (c, 0))
        x_shape = jax.ShapeDtypeStruct((L * nb, d), F32)
    rows = nb * lc
    const = lambda shape: pl.BlockSpec(shape, lambda c: (0,) * len(shape))
    mod3 = lambda t: t.reshape(1, nb, d)
    st_shape = jax.ShapeDtypeStruct((nb, ns), F32)
    kern = functools.partial(_s5_kernel, nb=nb, lc=lc, batch_major=batch_major)
    return pl.pallas_call(
        kern,
        grid=(L // lc,),
        in_specs=[x_spec, const((1, 1, d)), const((1, nb, d)), const((1, nb, d)), const((1, nb, d)),
                  const((nb, ns)), const((nb, ns)), const(win.shape), const(bre.shape), const(bim.shape),
                  const(cre.shape), const(cim.shape), const((1, ns)), const((1, ns)), const((1, d)),
                  const(wglu.shape)],
        out_specs=(x_spec, const((nb, ns)), const((nb, ns))),
        out_shape=(x_shape, st_shape, st_shape),
        scratch_shapes=[pltpu.VMEM((rows, d), F32), pltpu.VMEM((rows, d), F32),
                        pltpu.VMEM((rows, ns), F32), pltpu.VMEM((rows, ns), F32),
                        pltpu.VMEM((rows, d), F32)],
        compiler_params=_cparams(("arbitrary",), 56),
        name="s5",
    )(x, g.reshape(1, 1, d), mod3(sh), mod3(sc), mod3(gm), s0r, s0i, win, bre, bim, cre, cim,
      ar.reshape(1, ns), ai.reshape(1, ns), dsk.reshape(1, d), wglu)


def _router_kernel(x_ref, g_ref, sh_ref, sc_ref, wrt_ref, br_ref, cin_ref, h_ref, ti_ref, tg_ref, rk_ref, cnt_ref):
    @pl.when((pl.program_id(0) == 0) & (pl.program_id(1) == 0))
    def _():
        cnt_ref[...] = cin_ref[...]

    x = x_ref[0]
    h = _norm_mod(x, g_ref[0], sc_ref[0], sh_ref[0])
    hb = h.astype(BF16)
    h_ref[...] = h
    h_lo = (h - hb.astype(F32)).astype(BF16)
    lt = (_dot_nt(wrt_ref[0], hb) + _dot_nt(wrt_ref[0], h_lo) + _dot_nt(wrt_ref[1], hb)
          + br_ref[...])
    ne = lt.shape[0]
    idx = lax.broadcasted_iota(jnp.int32, lt.shape, 0)
    vals, ids = [], []
    for _ in range(TOP_K):
        m = jnp.max(lt, axis=0, keepdims=True)
        i = jnp.min(jnp.where(lt == m, idx, ne), axis=0, keepdims=True)
        vals.append(m)
        ids.append(i)
        lt = jnp.where(idx == i, -jnp.inf, lt)
    es = [jnp.exp(v - vals[0]) for v in vals]
    tot = es[0] + es[1] + es[2] + es[3]
    rows = lt.shape[1]
    ti_ref[...] = jnp.concatenate(ids, axis=0)
    tg_ref[...] = jnp.concatenate([e / tot for e in es] + [jnp.zeros((SUBLANES - TOP_K, rows), F32)], axis=0)

    upper = jnp.where(lax.broadcasted_iota(jnp.int32, (rows, rows), 0) < lax.broadcasted_iota(jnp.int32, (rows, rows), 1),
                      1.0, 0.0).astype(BF16)
    run = cnt_ref[...]
    ranks = []
    for k in range(TOP_K):
        oh = jnp.where(idx == ids[k], 1.0, 0.0)
        pre = _dot(oh.astype(BF16), upper)
        ranks.append(jnp.sum(oh * (pre + run), axis=0, keepdims=True))
        run = run + jnp.sum(oh, axis=1, keepdims=True)
    cnt_ref[...] = run
    rk_ref[...] = jnp.concatenate(ranks, axis=0).astype(jnp.int32)


def _router(x3, g, sh3, sc3, wrt, br, cnt_in, *, tl):
    nbx, L, d = x3.shape
    lm = sh3.shape[1]
    ne = wrt.shape[1]
    nt = L // tl
    t = nbx * L
    mod_spec = (pl.BlockSpec((1, 1, d), lambda b, c: (b, 0, 0)) if lm == 1
                else pl.BlockSpec((1, tl, d), lambda b, c: (b, c, 0)))
    tok = lambda rows: pl.BlockSpec((rows, tl), lambda b, c: (0, b * nt + c))
    return pl.pallas_call(
        _router_kernel,
        grid=(nbx, nt),
        in_specs=[pl.BlockSpec((1, tl, d), lambda b, c: (b, c, 0)),
                  pl.BlockSpec((1, 1, d), lambda b, c: (0, 0, 0)), mod_spec, mod_spec,
                  pl.BlockSpec((2, ne, d), lambda b, c: (0, 0, 0)), pl.BlockSpec((ne, 1), lambda b, c: (0, 0)),
                  pl.BlockSpec((ne, 1), lambda b, c: (0, 0))],
        out_specs=(pl.BlockSpec((tl, d), lambda b, c: (b * nt + c, 0)), tok(TOP_K), tok(SUBLANES), tok(TOP_K),
                   pl.BlockSpec((ne, 1), lambda b, c: (0, 0))),
        out_shape=(jax.ShapeDtypeStruct((t, d), F32), jax.ShapeDtypeStruct((TOP_K, t), jnp.int32),
                   jax.ShapeDtypeStruct((SUBLANES, t), F32), jax.ShapeDtypeStruct((TOP_K, t), jnp.int32),
                   jax.ShapeDtypeStruct((ne, 1), F32)),
        compiler_params=_cparams(("arbitrary", "arbitrary"), 32),
        name="router",
    )(x3, g.reshape(1, 1, d), sh3, sc3, wrt, br, cnt_in)


def _pos_kernel(cnt_ref, ti_ref, rk_ref, pos_ref, *, tm):
    cnt = cnt_ref[...]
    ne = cnt.shape[0]
    tiles = jnp.floor((cnt + (tm - 1)) * (1.0 / tm))
    low = jnp.where(lax.broadcasted_iota(jnp.int32, (ne, ne), 1) < lax.broadcasted_iota(jnp.int32, (ne, ne), 0),
                    1.0, 0.0).astype(BF16)
    start = _dot(low, jnp.broadcast_to(tiles, (ne, LANES)).astype(BF16))[:, 0:1] * tm
    ti = ti_ref[...]
    idx = lax.broadcasted_iota(jnp.int32, (ne, ti.shape[1]), 0)
    sel = [jnp.sum(jnp.where(idx == ti[k:k + 1, :], start, 0.0), axis=0, keepdims=True) for k in range(TOP_K)]
    pos_ref[...] = jnp.concatenate(sel, axis=0).astype(jnp.int32) + rk_ref[...]


def _pos(cnt, ti, rk, *, tm, tl):
    k, t = ti.shape
    ne = cnt.shape[0]
    tok = pl.BlockSpec((k, tl), lambda i: (0, i))
    return pl.pallas_call(
        functools.partial(_pos_kernel, tm=tm),
        grid=(t // tl,),
        in_specs=[pl.BlockSpec((ne, 1), lambda i: (0, 0)), tok, tok],
        out_specs=tok,
        out_shape=jax.ShapeDtypeStruct((k, t), jnp.int32),
        compiler_params=_cparams(("arbitrary",), 32),
        name="moe_pos",
    )(cnt, ti, rk)


def _plan_kernel(cnt_ref, te_ref, tv_ref, tf_ref, *, tm, ne, n_tiles):
    shift = tm.bit_length() - 1

    def body(i, carry):
        e, end = carry

        def adv(c):
            e2 = c[0] + 1
            return e2, c[1] + ((cnt_ref[e2] + (tm - 1)) >> shift)

        e2, end2 = lax.while_loop(lambda c: (c[1] <= i) & (c[0] < ne - 1), adv, (e, end))
        te_ref[i] = e2
        tv_ref[i] = (i < end2).astype(jnp.int32)
        tf_ref[i] = ((e2 != e) | (i == 0)).astype(jnp.int32)
        return e2, end2

    lax.fori_loop(0, n_tiles, body, (jnp.int32(0), (cnt_ref[0] + (tm - 1)) >> shift))


def _plan(cnt_i32, *, tm, n_tiles):
    ne = cnt_i32.shape[0]
    smem = pl.BlockSpec(memory_space=pltpu.SMEM)
    out = jax.ShapeDtypeStruct((n_tiles,), jnp.int32)
    return pl.pallas_call(
        functools.partial(_plan_kernel, tm=tm, ne=ne, n_tiles=n_tiles),
        in_specs=[smem], out_specs=(smem, smem, smem), out_shape=(out, out, out), name="moe_plan",
    )(cnt_i32)


def _dispatch_kernel(pos_ref, h_ref, xs_in, xs_out, sem, *, tl):
    del xs_in

    def body(j, c):
        for u in range(SUBLANES):
            r = j * SUBLANES + u
            for k in range(TOP_K):
                pltpu.make_async_copy(h_ref.at[pl.ds(r, 1), :], xs_out.at[pl.ds(pos_ref[k, r], 1), :], sem).start()
        return c

    lax.fori_loop(0, tl // SUBLANES, body, 0)
    for k in range(TOP_K):
        pltpu.make_async_copy(h_ref, xs_out.at[pl.ds(0, tl), :], sem).wait()


def _dispatch(pos, h, xs, *, tl):
    t, d = h.shape
    return pl.pallas_call(
        functools.partial(_dispatch_kernel, tl=tl),
        grid=(t // tl,),
        in_specs=[pl.BlockSpec((TOP_K, tl), lambda i: (0, i), memory_space=pltpu.SMEM),
                  pl.BlockSpec((tl, d), lambda i: (i, 0)), pl.BlockSpec(memory_space=pl.ANY)],
        out_specs=pl.BlockSpec(memory_space=pl.ANY),
        out_shape=jax.ShapeDtypeStruct(xs.shape, xs.dtype),
        scratch_shapes=[pltpu.SemaphoreType.DMA],
        input_output_aliases={2: 0},
        compiler_params=_cparams(("arbitrary",), 32),
        name="moe_dispatch",
    )(pos, h, xs)


def _combine_kernel(pos_ref, x_ref, gf_ref, tg_ref, fg_ref, outs_hbm, xo_ref, buf, sem, *, tl, final):
    nh = 2 if tl % (2 * SUBLANES) == 0 else 1
    hl = tl // nh
    for hf in range(nh):
        def body(j, c, hf=hf):
            for u in range(SUBLANES):
                r = hf * hl + j * SUBLANES + u
                for k in range(TOP_K):
                    pltpu.make_async_copy(outs_hbm.at[pl.ds(pos_ref[k, r], 1), :], buf.at[k, pl.ds(r, 1), :],
                                          sem.at[hf]).start()
            return c

        lax.fori_loop(0, hl // SUBLANES, body, 0)

    gates = jnp.concatenate([tg_ref[...], jnp.zeros((LANES - SUBLANES, tl), F32)], axis=0).T
    for hf in range(nh):
        rs = slice(hf * hl, (hf + 1) * hl)
        for k in range(TOP_K):
            pltpu.make_async_copy(outs_hbm.at[pl.ds(0, hl), :], buf.at[k, rs, :], sem.at[hf]).wait()
        y = gates[rs, 0:1] * buf[0, rs, :]
        for k in range(1, TOP_K):
            y = y + gates[rs, k:k + 1] * buf[k, rs, :]
        gf = gf_ref[0] if gf_ref.shape[1] == 1 else gf_ref[0, rs, :]
        xn = x_ref[0, rs, :] + gf * y
        if final:
            xn = xn * lax.rsqrt(jnp.mean(xn * xn, axis=-1, keepdims=True) + EPS) * fg_ref[...]
        xo_ref[0, rs, :] = xn


def _combine(pos, x3, gf3, tg, fgain, outs, *, tl, pos_off, final):
    nbx, L, d = x3.shape
    nt = L // tl
    lm = gf3.shape[1]
    mod_spec = (pl.BlockSpec((1, 1, d), lambda b, c: (b, 0, 0)) if lm == 1
                else pl.BlockSpec((1, tl, d), lambda b, c: (b, c, 0)))
    return pl.pallas_call(
        functools.partial(_combine_kernel, tl=tl, final=final),
        grid=(nbx, nt),
        in_specs=[pl.BlockSpec((TOP_K, tl), lambda b, c: (0, pos_off + b * nt + c), memory_space=pltpu.SMEM),
                  pl.BlockSpec((1, tl, d), lambda b, c: (b, c, 0)), mod_spec,
                  pl.BlockSpec((SUBLANES, tl), lambda b, c: (0, b * nt + c)),
                  pl.BlockSpec((1, d), lambda b, c: (0, 0)), pl.BlockSpec(memory_space=pl.ANY)],
        out_specs=pl.BlockSpec((1, tl, d), lambda b, c: (b, c, 0)),
        out_shape=jax.ShapeDtypeStruct((nbx, L, d), F32),
        scratch_shapes=[pltpu.VMEM((TOP_K, tl, d), F32), pltpu.SemaphoreType.DMA((2,))],
        compiler_params=_cparams(("arbitrary", "arbitrary"), 40),
        name="moe_combine",
    )(pos, x3, gf3, tg, fgain.reshape(1, d), outs)


def _gmm_kernel(te_ref, tv_ref, tf_ref, x_ref, wgu_ref, bgu_ref, wd_ref, bd_ref, o_ref,
                wgu_s, wd_s, act_s):
    i = pl.program_id(0)
    de = wd_s.shape[0]
    nch = 4
    cw = de // nch

    @pl.when(tf_ref[i] == 1)
    def _():
        for n in range(2 * nch):
            wgu_s[:, n * cw:(n + 1) * cw] = wgu_ref[0, :, n * cw:(n + 1) * cw].astype(BF16)
        for n in range(nch):
            wd_s[n * cw:(n + 1) * cw, :] = wd_ref[0, n * cw:(n + 1) * cw, :].astype(BF16)

    @pl.when(tv_ref[i] == 1)
    def _():
        x = x_ref[...].astype(BF16)
        for n in range(nch):
            glu = _dot(x, wgu_s[:, n * cw:(n + 1) * cw]) + bgu_ref[0, :, n * cw:(n + 1) * cw]
            lin = _dot(x, wgu_s[:, de + n * cw:de + (n + 1) * cw]) + bgu_ref[0, :, de + n * cw:de + (n + 1) * cw]
            glu = jnp.minimum(glu, SWIGLU_LIMIT)
            lin = jnp.clip(lin, -SWIGLU_LIMIT, SWIGLU_LIMIT)
            act = glu * jax.nn.sigmoid(SWIGLU_ALPHA * glu) * (lin + 1.0)
            act_s[:, n * cw:(n + 1) * cw] = act.astype(BF16)
        o_ref[...] = _dot(act_s[...], wd_s[...]) + bd_ref[0]

    @pl.when(tv_ref[i] == 0)
    def _():
        o_ref[...] = jnp.zeros_like(o_ref)


def _gmm(xs, tile_expert, tile_valid, tile_first, w_gu, b_gu, w_down, b_down, *, tm, layer):
    n_slots, d = xs.shape
    depth, ne, _, de2 = w_gu.shape
    de = de2 // 2
    n_tiles = n_slots // tm
    b_gu = b_gu.reshape(depth * ne, 1, de2)
    b_down = b_down.reshape(depth * ne, 1, d)
    grid_spec = pltpu.PrefetchScalarGridSpec(
        num_scalar_prefetch=3,
        grid=(n_tiles,),
        in_specs=[pl.BlockSpec((tm, d), lambda i, te, tv, tf: (i, 0)),
                  pl.BlockSpec((None, 1, d, de2), lambda i, te, tv, tf: (layer, te[i], 0, 0)),
                  pl.BlockSpec((1, 1, de2), lambda i, te, tv, tf: (layer * ne + te[i], 0, 0)),
                  pl.BlockSpec((None, 1, de, d), lambda i, te, tv, tf: (layer, te[i], 0, 0)),
                  pl.BlockSpec((1, 1, d), lambda i, te, tv, tf: (layer * ne + te[i], 0, 0))],
        out_specs=pl.BlockSpec((tm, d), lambda i, te, tv, tf: (i, 0)),
        scratch_shapes=[pltpu.VMEM((d, de2), BF16), pltpu.VMEM((de, d), BF16), pltpu.VMEM((tm, de), BF16)],
    )
    return pl.pallas_call(
        _gmm_kernel,
        grid_spec=grid_spec,
        out_shape=jax.ShapeDtypeStruct((n_slots, d), F32),
        compiler_params=_cparams(("arbitrary",), 56),
        name="gmm",
    )(tile_expert, tile_valid, tile_first, xs, w_gu, b_gu, w_down, b_down)


def _moe_layer(xp, xs_, g, mods_p, mods_s, wr, br_, w_gu, b_gu, w_down, b_down, slots, fgain, *, layer, final):
    bp, seq, d = xp.shape
    bs = xs_.shape[0]
    ne = wr.shape[1]
    tm = GMM_ROWS
    n_tiles = slots.shape[0] // tm
    sh_p, sc_p, gf_p = mods_p
    sh_s, sc_s, gf_s = mods_s
    wr_t = wr.T
    wr_hi = wr_t.astype(BF16)
    wrt = jnp.stack([wr_hi, (wr_t - wr_hi.astype(F32)).astype(BF16)])
    br = br_.reshape(ne, 1)
    cnt0 = jnp.zeros((ne, 1), F32)
    hp, tip, tgp, rkp, cnt1 = _router(xp, g, sh_p[:, None], sc_p[:, None], wrt, br, cnt0, tl=ROUTER_ROWS)
    hs, tis, tgs, rks, cnt = _router(xs_[None], g, sh_s[None], sc_s[None], wrt, br, cnt1, tl=bs)
    pos_p = _pos(cnt, tip, rkp, tm=tm, tl=ROUTER_ROWS)
    pos_s = _pos(cnt, tis, rks, tm=tm, tl=bs)
    te, tv, tf = _plan(cnt.reshape(ne).astype(jnp.int32), tm=tm, n_tiles=n_tiles)
    slots = _dispatch(pos_p, hp, slots, tl=ROUTER_ROWS)
    slots = _dispatch(pos_s, hs, slots, tl=bs)
    outs = _gmm(slots, te, tv, tf, w_gu, b_gu, w_down, b_down, tm=tm, layer=layer)
    xp = _combine(pos_p, xp, gf_p[:, None], tgp, fgain, outs, tl=ROUTER_ROWS, pos_off=0, final=final)
    xs_ = _combine(pos_s, xs_[None], gf_s[None], tgs, fgain, outs, tl=bs, pos_off=0, final=final)[0]
    return xp, xs_, slots


def _mlstm_kernel(x_ref, g_ref, sh_ref, sc_ref, gm_ref, wup_ref, cw_ref, cb_ref, wq_ref, wk_ref, wv_ref,
                  wg_ref, bg_ref, lnw_ref, skip_ref, wdn_ref,
                  xo_ref, cst_ref, nst_ref, mst_ref, conv_ref,
                  xm_s, up_s, xc_s, q_s, k_s, v_s, *, lc, heads, dh):
    c = pl.program_id(1)
    inner = heads * dh
    pad = SUBLANES

    @pl.when(c == 0)
    def _():
        cst_ref[...] = jnp.zeros_like(cst_ref)
        nst_ref[...] = jnp.zeros_like(nst_ref)
        mst_ref[...] = jnp.zeros_like(mst_ref)
        xm_s[0:pad, :] = jnp.zeros((pad, inner), F32)

    x = x_ref[0]
    h = _norm_mod(x, g_ref[0], sc_ref[0], sh_ref[0]).astype(BF16)
    up_s[...] = _dot(h, wup_ref[...])
    xm_s[pad:pad + lc, :] = up_s[:, :inner]
    conv = cb_ref[...]
    for k in range(ML_CONV):
        off = pad - (ML_CONV - 1) + k
        conv = conv + cw_ref[k:k + 1, :] * xm_s[off:off + lc, :]
    xc_s[...] = _silu(conv)
    xm_s[0:pad, :] = xm_s[lc:lc + pad, :]
    conv_ref[0] = xm_s[0:pad, :]

    gacc = jnp.broadcast_to(bg_ref[...], (lc, LANES))
    for hd in range(heads):
        hs = slice(hd * dh, (hd + 1) * dh)
        xch = xc_s[:, hs].astype(BF16)
        q = _dot(xch, wq_ref[hd])
        k = _dot(xch, wk_ref[hd]) * (dh ** -0.5)
        v = _dot(up_s[:, hs].astype(BF16), wv_ref[hd])
        qb = q.astype(BF16)
        vb = v.astype(BF16)
        q_s[:, hs] = qb
        k_s[:, hs] = k
        v_s[:, hs] = vb
        gacc = gacc + _dot(qb, wg_ref[0, hd]) + _dot(k.astype(BF16), wg_ref[1, hd]) + _dot(vb, wg_ref[2, hd])

    lf = _log_sigmoid(gacc)
    row = lax.broadcasted_iota(jnp.int32, (lc, lc), 0)
    col = lax.broadcasted_iota(jnp.int32, (lc, lc), 1)
    causal = row >= col
    tri = jnp.where(causal, 1.0, 0.0).astype(BF16)
    hi = lf.astype(BF16)
    r1 = lf - hi.astype(F32)
    mid = r1.astype(BF16)
    lo = (r1 - mid.astype(F32)).astype(BF16)
    bcum = _dot(tri, hi) + _dot(tri, mid) + _dot(tri, lo)
    g_t = gacc.T
    b_t = bcum.T

    acc = jnp.zeros((lc, xo_ref.shape[2]), F32)
    for hd in range(heads):
        hs = slice(hd * dh, (hd + 1) * dh)
        ig_c = gacc[:, hd:hd + 1]
        b_c = bcum[:, heads + hd:heads + hd + 1]
        ig_r = g_t[hd:hd + 1, :]
        b_r = b_t[heads + hd:heads + hd + 1, :]
        m_prev = mst_ref[0, hd][:, 0:1]
        dm = jnp.where(causal, (b_c - b_r) + ig_r, -jnp.inf)
        inter = b_c + m_prev
        m_t = jnp.maximum(inter, jnp.max(dm, axis=1, keepdims=True))
        w_intra = jnp.exp(dm - m_t)
        w_inter = jnp.exp(inter - m_t)
        qb = q_s[:, hs]
        kf = k_s[:, hs]
        vb = v_s[:, hs]
        cmat = cst_ref[0, hd]
        nvec = nst_ref[0, hd]
        qk = _dot_nt(qb, kf.astype(BF16)) * w_intra
        num = _dot(qk.astype(BF16), vb) + w_inter * _dot(qb, cmat.astype(BF16))
        qn = jnp.sum(qb.astype(F32) * nvec.astype(BF16).astype(F32), axis=1, keepdims=True)
        den = jnp.sum(qk, axis=1, keepdims=True) + w_inter * qn
        hh = num / jnp.maximum(jnp.abs(den), jnp.exp(-m_t))

        bl = b_c[lc - 1:lc, :]
        m_new = m_t[lc - 1:lc, :]
        w_s = jnp.exp(bl - b_c + ig_c - m_new)
        decay = jnp.exp(bl + m_prev - m_new)
        kw = w_s * kf
        cst_ref[0, hd] = decay * cmat + _dot(kw.T.astype(BF16), vb)
        nst_ref[0, hd] = decay * nvec + jnp.sum(kw, axis=0, keepdims=True)
        mst_ref[0, hd] = jnp.broadcast_to(m_new, (1, LANES))

        mu = jnp.mean(hh, axis=1, keepdims=True)
        hc = hh - mu
        var = jnp.mean(hc * hc, axis=1, keepdims=True)
        hn = hc * lax.rsqrt(var + EPS) * lnw_ref[:, hs]
        o = (hn + skip_ref[:, hs] * xc_s[:, hs]) * _silu(up_s[:, inner + hd * dh:inner + (hd + 1) * dh])
        acc = acc + _dot(o.astype(BF16), wdn_ref[hs, :])
    xo_ref[0] = x + gm_ref[0] * acc


def _mlstm_prompt(x, g, sh, sc, gm, wts, *, lc):
    wup, cw, cb, wq, wk, wv, wg, bg, lnw, skip, wdn = wts
    nb, L, d = x.shape
    heads, dh, _ = wq.shape
    inner = heads * dh
    const = lambda shape: pl.BlockSpec(shape, lambda b, c: (0,) * len(shape), pipeline_mode=pl.Buffered(1))
    per_b = lambda shape: pl.BlockSpec(shape, lambda b, c: (b,) + (0,) * (len(shape) - 1))
    mod3 = lambda t: t.reshape(nb, 1, d)
    kern = functools.partial(_mlstm_kernel, lc=lc, heads=heads, dh=dh)
    return pl.pallas_call(
        kern,
        grid=(nb, L // lc),
        in_specs=[pl.BlockSpec((1, lc, d), lambda b, c: (b, c, 0)), const((1, 1, d)),
                  per_b((1, 1, d)), per_b((1, 1, d)), per_b((1, 1, d)),
                  const(wup.shape), const(cw.shape), const((1, inner)), const(wq.shape), const(wk.shape),
                  const(wv.shape), const(wg.shape), const((1, LANES)), const((1, inner)), const((1, inner)),
                  const(wdn.shape)],
        out_specs=(pl.BlockSpec((1, lc, d), lambda b, c: (b, c, 0)),
                   per_b((1, heads, dh, dh)), per_b((1, heads, 1, dh)), per_b((1, heads, 1, LANES)),
                   per_b((1, SUBLANES, inner))),
        out_shape=(jax.ShapeDtypeStruct((nb, L, d), F32),
                   jax.ShapeDtypeStruct((nb, heads, dh, dh), F32),
                   jax.ShapeDtypeStruct((nb, heads, 1, dh), F32),
                   jax.ShapeDtypeStruct((nb, heads, 1, LANES), F32),
                   jax.ShapeDtypeStruct((nb, SUBLANES, inner), F32)),
        scratch_shapes=[pltpu.VMEM((lc + SUBLANES, inner), F32), pltpu.VMEM((lc, 2 * inner), F32),
                        pltpu.VMEM((lc, inner), F32), pltpu.VMEM((lc, inner), BF16),
                        pltpu.VMEM((lc, inner), F32), pltpu.VMEM((lc, inner), BF16)],
        compiler_params=_cparams(("arbitrary", "arbitrary"), 58),
        name="mlstm",
    )(x, g.reshape(1, 1, d), mod3(sh), mod3(sc), mod3(gm), wup, cw, cb.reshape(1, inner), wq, wk, wv, wg, bg,
      lnw.reshape(1, inner), skip.reshape(1, inner), wdn)


def _mls_proj_kernel(x_ref, g_ref, sh_ref, sc_ref, wxm_ref, wz_ref, buf_ref, cw_ref, cb_ref, wq_ref, wk_ref,
                     wv_ref, wg_ref, bg_ref,
                     q_ref, k_ref, v_ref, xc_ref, z_ref, gate_ref, nbuf_ref):
    hd = pl.program_id(0)

    @pl.when(hd == 0)
    def _():
        gate_ref[...] = jnp.broadcast_to(bg_ref[...], gate_ref.shape)

    h = _norm_mod(x_ref[...], g_ref[...], sc_ref[...], sh_ref[...]).astype(BF16)
    xm = _dot(h, wxm_ref[...])
    z_ref[...] = _dot(h, wz_ref[...])
    conv = cb_ref[...] + cw_ref[ML_CONV - 1:ML_CONV, :] * xm
    for k in range(ML_CONV - 1):
        conv = conv + cw_ref[k:k + 1, :] * buf_ref[k]
    for k in range(ML_CONV - 2):
        nbuf_ref[k] = buf_ref[k + 1]
    nbuf_ref[ML_CONV - 2] = xm
    xc = _silu(conv)
    xc_ref[...] = xc
    xcb = xc.astype(BF16)
    dh = xm.shape[1]
    q = _dot(xcb, wq_ref[0])
    k = _dot(xcb, wk_ref[0]) * (dh ** -0.5)
    v = _dot(xm.astype(BF16), wv_ref[0])
    q_ref[...] = q
    k_ref[...] = k
    v_ref[...] = v
    gate_ref[...] += (_dot(q.astype(BF16), wg_ref[0, 0]) + _dot(k.astype(BF16), wg_ref[1, 0])
                      + _dot(v.astype(BF16), wg_ref[2, 0]))


def _mls_proj(x, g, sh, sc, buf, wts):
    wup, cw, cb, wq, wk, wv, wg, bg = wts
    nb, d = x.shape
    heads, dh, _ = wq.shape
    inner = heads * dh
    full = lambda shape: pl.BlockSpec(shape, lambda h: (0,) * len(shape))
    colblk = pl.BlockSpec((nb, dh), lambda h: (0, h))
    act = jax.ShapeDtypeStruct((nb, inner), F32)
    return pl.pallas_call(
        _mls_proj_kernel,
        grid=(heads,),
        in_specs=[full((nb, d)), full((1, d)), full((nb, d)), full((nb, d)),
                  pl.BlockSpec((d, dh), lambda h: (0, h)), pl.BlockSpec((d, dh), lambda h: (0, heads + h)),
                  pl.BlockSpec((ML_CONV - 1, nb, dh), lambda h: (0, 0, h)),
                  pl.BlockSpec((ML_CONV, dh), lambda h: (0, h)), pl.BlockSpec((1, dh), lambda h: (0, h)),
                  pl.BlockSpec((1, dh, dh), lambda h: (h, 0, 0)), pl.BlockSpec((1, dh, dh), lambda h: (h, 0, 0)),
                  pl.BlockSpec((1, dh, dh), lambda h: (h, 0, 0)),
                  pl.BlockSpec((3, 1, dh, LANES), lambda h: (0, h, 0, 0)), full((1, LANES))],
        out_specs=(colblk, colblk, colblk, colblk, colblk, full((nb, LANES)),
                   pl.BlockSpec((ML_CONV - 1, nb, dh), lambda h: (0, 0, h))),
        out_shape=(act, act, act, act, act, jax.ShapeDtypeStruct((nb, LANES), F32),
                   jax.ShapeDtypeStruct((ML_CONV - 1, nb, inner), F32)),
        compiler_params=_cparams(("arbitrary",), 32),
        name="mls_proj",
    )(x, g.reshape(1, d), sh, sc, wup, wup, buf, cw, cb.reshape(1, inner), wq, wk, wv, wg, bg)


def _mls_gate_scalars(ig, fg, m0):
    lf = _log_sigmoid(fg)
    inter = lf + m0
    m_t = jnp.maximum(inter, ig)
    return m_t, jnp.exp(ig - m_t), jnp.exp(inter - m_t)


def _mls_state_kernel(q_ref, k_ref, v_ref, ig_ref, fg_ref, m0_ref, c_ref, qc_ref, cn_ref, *, heads, dh):
    _, w_in, w_dec = _mls_gate_scalars(ig_ref[0], fg_ref[0], m0_ref[0])
    rowmask = lax.broadcasted_iota(jnp.int32, (LANES, dh), 0) == 0
    for hd in range(heads):
        hs = slice(hd * dh, (hd + 1) * dh)
        wi = w_in[:, hd:hd + 1]
        wd = w_dec[:, hd:hd + 1]
        cmat = c_ref[0, hd]
        q8 = jnp.broadcast_to(q_ref[0, :, hs], (SUBLANES, dh)).astype(BF16)
        qc_ref[0, :, hs] = _dot(q8, cmat.astype(BF16))[0:1, :]
        kw = jnp.where(rowmask, wi * k_ref[0, :, hs], 0.0)
        vv = jnp.where(rowmask, v_ref[0, :, hs], 0.0)
        cn_ref[0, hd] = wd * cmat + _dot(kw.T.astype(BF16), vv.astype(BF16))


def _mls_state(q, k, v, ig, fg, m0, cst):
    nb, inner = q.shape
    _, heads, dh, _ = cst.shape
    row = lambda w: pl.BlockSpec((1, 1, w), lambda b: (b, 0, 0))
    r3 = lambda t: t.reshape(nb, 1, t.shape[1])
    kern = functools.partial(_mls_state_kernel, heads=heads, dh=dh)
    qc, cn = pl.pallas_call(
        kern,
        grid=(nb,),
        in_specs=[row(inner), row(inner), row(inner), row(LANES), row(LANES), row(LANES),
                  pl.BlockSpec((1, heads, dh, dh), lambda b: (b, 0, 0, 0))],
        out_specs=(row(inner), pl.BlockSpec((1, heads, dh, dh), lambda b: (b, 0, 0, 0))),
        out_shape=(jax.ShapeDtypeStruct((nb, 1, inner), F32), jax.ShapeDtypeStruct(cst.shape, F32)),
        compiler_params=_cparams(("arbitrary",), 40),
        name="mls_state",
    )(r3(q), r3(k), r3(v), r3(ig), r3(fg), r3(m0), cst)
    return qc.reshape(nb, inner), cn


def _mls_post_kernel(x_ref, gm_ref, q_ref, k_ref, v_ref, xc_ref, z_ref, qc_ref, n_ref, ig_ref, fg_ref, m0_ref,
                     lnw_ref, skip_ref, wdn_ref, xo_ref, nn_ref, mn_ref, *, heads, dh):
    m_t, w_in, w_dec = _mls_gate_scalars(ig_ref[...], fg_ref[...], m0_ref[...])
    mn_ref[...] = m_t
    acc = jnp.zeros(xo_ref.shape, F32)
    for hd in range(heads):
        hs = slice(hd * dh, (hd + 1) * dh)
        wi = w_in[:, hd:hd + 1]
        wd = w_dec[:, hd:hd + 1]
        mt = m_t[:, hd:hd + 1]
        q = q_ref[:, hs]
        k = k_ref[:, hs]
        qr = q.astype(BF16).astype(F32)
        qk = jnp.sum(qr * k.astype(BF16).astype(F32), axis=1, keepdims=True) * wi
        nvec = n_ref[:, hs]
        num = qk.astype(BF16).astype(F32) * v_ref[:, hs].astype(BF16).astype(F32) + wd * qc_ref[:, hs]
        den = qk + wd * jnp.sum(qr * nvec.astype(BF16).astype(F32), axis=1, keepdims=True)
        hh = num / jnp.maximum(jnp.abs(den), jnp.exp(-mt))
        nn_ref[:, hs] = wd * nvec + wi * k
        mu = jnp.mean(hh, axis=1, keepdims=True)
        hc = hh - mu
        var = jnp.mean(hc * hc, axis=1, keepdims=True)
        hn = hc * lax.rsqrt(var + EPS) * lnw_ref[:, hs]
        o = (hn + skip_ref[:, hs] * xc_ref[:, hs]) * _silu(z_ref[:, hs])
        acc = acc + _dot(o.astype(BF16), wdn_ref[hs, :])
    xo_ref[...] = x_ref[...] + gm_ref[...] * acc


def _mls_post(x, gm, q, k, v, xc, z, qc, n0, ig, fg, m0, lnw, skip, wdn, *, heads, dh):
    nb, d = x.shape
    inner = heads * dh
    kern = functools.partial(_mls_post_kernel, heads=heads, dh=dh)
    return pl.pallas_call(
        kern,
        out_shape=(jax.ShapeDtypeStruct((nb, d), F32), jax.ShapeDtypeStruct((nb, inner), F32),
                   jax.ShapeDtypeStruct((nb, LANES), F32)),
        compiler_params=pltpu.CompilerParams(vmem_limit_bytes=40 * MIB),
        name="mls_post",
    )(x, gm, q, k, v, xc, z, qc, n0, ig, fg, m0, lnw.reshape(1, inner), skip.reshape(1, inner), wdn)


def _pad_lanes(t):
    return jnp.pad(t, ((0, 0), (0, LANES - t.shape[1])))


def kernel(x_prompt, x_sample, c_prompt, c_sample, state_s5_re, state_s5_im, state_mlstm_C, state_mlstm_n, state_mlstm_m, state_mlstm_conv, norm_mix_g, norm_ffn_g, final_norm_g, ada_w, ada_b, s5_w_in, s5_lam_re, s5_lam_im, s5_log_dt, s5_b_re, s5_b_im, s5_c_re, s5_c_im, s5_d, s5_w_glu, ml_w_up, ml_conv_w, ml_conv_b, ml_w_q, ml_w_k, ml_w_v, ml_w_gate, ml_b_gate, ml_ln_w, ml_skip, ml_w_down, moe_w_router, moe_b_router, moe_w_gu, moe_b_gu, moe_w_down, moe_b_down):
    bp, seq, d = x_prompt.shape
    bs = x_sample.shape[0]
    tp = bp * seq
    _, groups, pstate, gch = s5_b_re.shape
    ns = groups * pstate
    heads, dh = ml_w_q.shape[1], ml_w_q.shape[2]
    inner = heads * dh
    ne = moe_w_router.shape[2]

    mods = _ada(jnp.concatenate([c_prompt, c_sample], axis=0), ada_w, ada_b)

    def mod(i, j):
        m = mods[i, :, j * d:(j + 1) * d]
        return m[:bp], m[bp:]

    ar, ai, bbr, bbi = _s5_prep(s5_lam_re[0], s5_lam_im[0], s5_log_dt[0], s5_b_re[0], s5_b_im[0])
    nblk = d // LANES
    bre = _block_diag(bbr.reshape(groups, gch, pstate), nblk).astype(BF16)
    bim = _block_diag(bbi.reshape(groups, gch, pstate), nblk).astype(BF16)
    cre = _block_diag(s5_c_re[0].transpose(0, 2, 1), nblk).astype(BF16)
    cim = _block_diag(s5_c_im[0].transpose(0, 2, 1), nblk).astype(BF16)
    s5_wts = (s5_w_in[0].astype(BF16), bre, bim, cre, cim, ar.reshape(ns), ai.reshape(ns), s5_d[0],
              s5_w_glu[0].astype(BF16))
    (sh_p, sh_s), (sc_p, sc_s), (gm_p, gm_s) = mod(0, 0), mod(0, 1), mod(0, 2)
    zeros_st = jnp.zeros((bp, ns), F32)
    xp, p_re, p_im = _s5_layer(x_prompt, norm_mix_g[0], sh_p, sc_p, gm_p, zeros_st, zeros_st, s5_wts,
                               batch_major=True, lc=S5_CHUNK)
    xs_, s_re, s_im = _s5_layer(x_sample.reshape(bs, d), norm_mix_g[0], sh_s, sc_s, gm_s,
                                state_s5_re[0].reshape(bs, ns), state_s5_im[0].reshape(bs, ns), s5_wts,
                                batch_major=False, lc=1)

    n_tiles = -(-(TOP_K * (tp + bs)) // GMM_ROWS) + ne
    slots = jnp.zeros((n_tiles * GMM_ROWS, d), F32)

    def moe_layer(i, xp, xs_, slots, final):
        (sh_p, sh_s), (sc_p, sc_s), (gf_p, gf_s) = mod(i, 3), mod(i, 4), mod(i, 5)
        return _moe_layer(xp, xs_, norm_ffn_g[i], (sh_p, sc_p, gf_p), (sh_s, sc_s, gf_s), moe_w_router[i],
                          moe_b_router[i], moe_w_gu, moe_b_gu, moe_w_down, moe_b_down, slots,
                          final_norm_g, layer=i, final=final)

    xp, xs_, slots = moe_layer(0, xp, xs_, slots, False)

    wg = _pad_lanes(ml_w_gate[0]).reshape(3, heads, dh, LANES).astype(BF16)
    bg = _pad_lanes(ml_b_gate[0][None])
    wup = ml_w_up[0].astype(BF16)
    wq, wk, wv = ml_w_q[0].astype(BF16), ml_w_k[0].astype(BF16), ml_w_v[0].astype(BF16)
    wdn = ml_w_down[0].astype(BF16)
    (sh_p, sh_s), (sc_p, sc_s), (gm_p, gm_s) = mod(1, 0), mod(1, 1), mod(1, 2)
    xp, p_c, p_n, p_m, p_tail = _mlstm_prompt(
        xp, norm_mix_g[1], sh_p, sc_p, gm_p,
        (wup, ml_conv_w[0], ml_conv_b[0], wq, wk, wv, wg, bg, ml_ln_w[0], ml_skip[0], wdn), lc=ML_CHUNK)
    p_n = p_n.reshape(bp, heads, dh)
    p_m = p_m[:, :, 0, 0]
    p_conv = p_tail[:, SUBLANES - (ML_CONV - 1):, :]

    buf = state_mlstm_conv[0].transpose(1, 0, 2)
    q, k, v, xc, z, gates, nbuf = _mls_proj(xs_, norm_mix_g[1], sh_s, sc_s, buf,
                                            (wup, ml_conv_w[0], ml_conv_b[0], wq, wk, wv, wg, bg))
    ig = _pad_lanes(gates[:, :heads])
    fg = _pad_lanes(gates[:, heads:2 * heads])
    m0 = _pad_lanes(state_mlstm_m[0])
    qc, s_c = _mls_state(q, k, v, ig, fg, m0, state_mlstm_C[0])
    xs_, s_n, s_m = _mls_post(xs_, gm_s, q, k, v, xc, z, qc, state_mlstm_n[0].reshape(bs, inner), ig, fg, m0,
                              ml_ln_w[0], ml_skip[0], wdn, heads=heads, dh=dh)
    s_n = s_n.reshape(bs, heads, dh)
    s_m = s_m[:, :heads]
    s_conv = nbuf.transpose(1, 0, 2)

    y_p, y_s, _ = moe_layer(1, xp, xs_, slots, True)
    y_s = y_s.reshape(bs, 1, d)
    return (y_p, y_s,
            p_re.reshape(1, bp, groups, pstate), p_im.reshape(1, bp, groups, pstate),
            p_c[None], p_n[None], p_m[None], p_conv[None],
            s_re.reshape(1, bs, groups, pstate), s_im.reshape(1, bs, groups, pstate),
            s_c[None], s_n[None], s_m[None], s_conv[None])
```

```python
import functools

import jax
import jax.numpy as jnp
from jax import lax
from jax.experimental import pallas as pl
from jax.experimental.pallas import tpu as pltpu

F32 = jnp.float32
BF16 = jnp.bfloat16
EPS = 1e-6
TOP_K = 4
SWIGLU_LIMIT = 7.0
SWIGLU_ALPHA = 1.702
ML_CONV = 4

LANES = 128
SUBLANES = 8
MIB = 1024 * 1024

S5_CHUNK = 64
ML_CHUNK = 256
ROUTER_ROWS = 512
GMM_ROWS = 512
NORM_ROWS = 512


def _cparams(semantics, vmem_mib):
    return pltpu.CompilerParams(dimension_semantics=semantics, vmem_limit_bytes=int(vmem_mib * MIB))


def _dot(a, b):
    return jnp.dot(a, b, preferred_element_type=F32)


def _dot_nt(a, b):
    return lax.dot_general(a, b, (((1,), (1,)), ((), ())), preferred_element_type=F32)


def _norm_mod(x, g, sc, sh):
    y = x * lax.rsqrt(jnp.mean(x * x, axis=-1, keepdims=True) + EPS)
    return (y * g) * (1.0 + sc) + sh


def _silu(x):
    return x * jax.nn.sigmoid(x)


def _log_sigmoid(x):
    return -(jnp.maximum(-x, 0.0) + jnp.log1p(jnp.exp(-jnp.abs(x))))


def _ada_kernel(c_ref, w_ref, b_ref, o_ref):
    c = c_ref[...]
    o_ref[0] = _dot(_silu(c).astype(BF16), w_ref[0].astype(BF16)) + b_ref[0]


def _ada(c_all, ada_w, ada_b):
    depth, d, n = ada_w.shape
    rows = c_all.shape[0]
    tn = n // 4
    return pl.pallas_call(
        _ada_kernel,
        grid=(depth, n // tn),
        in_specs=[pl.BlockSpec((rows, d), lambda i, j: (0, 0)),
                  pl.BlockSpec((1, d, tn), lambda i, j: (i, 0, j)),
                  pl.BlockSpec((1, 1, tn), lambda i, j: (i, 0, j))],
        out_specs=pl.BlockSpec((1, rows, tn), lambda i, j: (i, 0, j)),
        out_shape=jax.ShapeDtypeStruct((depth, rows, n), F32),
        compiler_params=_cparams(("arbitrary", "arbitrary"), 32),
        name="ada",
    )(c_all, ada_w, ada_b.reshape(depth, 1, n))


def _s5_prep_kernel(lr_ref, li_ref, ldt_ref, br_ref, bi_ref, ar_ref, ai_ref, bbr_ref, bbi_ref):
    dt = jnp.exp(ldt_ref[...])
    lr = lr_ref[...]
    li = li_ref[...]
    mag = jnp.exp(lr * dt)
    ar = mag * jnp.cos(li * dt)
    ai = mag * jnp.sin(li * dt)
    den = lr * lr + li * li
    nr = ar - 1.0
    wr = (nr * lr + ai * li) / den
    wi = (ai * lr - nr * li) / den
    br = br_ref[...]
    bi = bi_ref[...]
    ar_ref[...] = ar
    ai_ref[...] = ai
    bbr_ref[...] = wr * br - wi * bi
    bbi_ref[...] = wr * bi + wi * br


def _s5_prep(lam_re, lam_im, log_dt, b_re, b_im):
    g, p, c = b_re.shape
    rep = lambda t: jnp.repeat(t, c, axis=0)
    bt = lambda t: t.transpose(0, 2, 1).reshape(g * c, p)
    shp = jax.ShapeDtypeStruct((g * c, p), F32)
    ar, ai, bbr, bbi = pl.pallas_call(
        _s5_prep_kernel, out_shape=(shp, shp, shp, shp), name="s5_prep",
    )(rep(lam_re), rep(lam_im), rep(log_dt[:, None]), bt(b_re), bt(b_im))
    return ar[::c], ai[::c], bbr, bbi


def _block_diag(t, nblk):
    g, a, b = t.shape
    gl = g // nblk
    t4 = t.reshape(nblk, gl, a, b)
    eye = jnp.eye(gl, dtype=t.dtype)
    return jnp.einsum('jgab,gh->jgahb', t4, eye).reshape(nblk, gl * a, gl * b)


def _s5_kernel(x_ref, g_ref, sh_ref, sc_ref, gm_ref, s0r_ref, s0i_ref, win_ref, bre_ref, bim_ref,
               cre_ref, cim_ref, ar_ref, ai_ref, d_ref, wglu_ref,
               xo_ref, fr_ref, fi_ref,
               xt_ref, u_ref, sre_ref, sim_ref, y_ref, *, nb, lc, batch_major):
    c = pl.program_id(0)
    rows = nb * lc
    d = u_ref.shape[1]
    ns = sre_ref.shape[1]
    nblk = bre_ref.shape[0]
    kb = d // nblk
    sb = ns // nblk

    @pl.when(c == 0)
    def _():
        fr_ref[...] = s0r_ref[...]
        fi_ref[...] = s0i_ref[...]

    if batch_major:
        xt_ref[...] = jnp.swapaxes(x_ref[...], 0, 1).reshape(rows, d)
    else:
        xt_ref[...] = x_ref[...]
    x3 = xt_ref[...].reshape(lc, nb, d)
    h = _norm_mod(x3, g_ref[...], sc_ref[...], sh_ref[...]).reshape(rows, d).astype(BF16)
    u = _dot(h, win_ref[...])
    u_ref[...] = u
    ub = u.astype(BF16)
    for j in range(nblk):
        uj = ub[:, j * kb:(j + 1) * kb]
        sre_ref[:, j * sb:(j + 1) * sb] = _dot(uj, bre_ref[j])
        sim_ref[:, j * sb:(j + 1) * sb] = _dot(uj, bim_ref[j])

    cb = max(LANES, (4 * SUBLANES * LANES) // nb)
    unroll = 8 if lc % 8 == 0 else 1
    for k in range(ns // cb):
        cs = slice(k * cb, (k + 1) * cb)
        a_r = jnp.broadcast_to(ar_ref[:, cs], (nb, cb))
        a_i = jnp.broadcast_to(ai_ref[:, cs], (nb, cb))

        def body(i, carry, cs=cs, a_r=a_r, a_i=a_i):
            sr, si = carry
            for jj in range(unroll):
                r0 = pl.multiple_of((i * unroll + jj) * nb, nb)
                br = sre_ref[pl.ds(r0, nb), cs]
                bi = sim_ref[pl.ds(r0, nb), cs]
                nr = a_r * sr - a_i * si + br
                ni = a_r * si + a_i * sr + bi
                sre_ref[pl.ds(r0, nb), cs] = nr
                sim_ref[pl.ds(r0, nb), cs] = ni
                sr, si = nr, ni
            return sr, si

        sr, si = lax.fori_loop(0, lc // unroll, body, (fr_ref[:, cs], fi_ref[:, cs]))
        fr_ref[:, cs] = sr
        fi_ref[:, cs] = si

    for j in range(nblk):
        sr = sre_ref[:, j * sb:(j + 1) * sb].astype(BF16)
        si = sim_ref[:, j * sb:(j + 1) * sb].astype(BF16)
        yj = _dot(sr, cre_ref[j]) - _dot(si, cim_ref[j])
        y_ref[:, j * kb:(j + 1) * kb] = yj + d_ref[:, j * kb:(j + 1) * kb] * u_ref[:, j * kb:(j + 1) * kb]

    yg = jax.nn.gelu(y_ref[...]).astype(BF16)
    vg = _dot(yg, wglu_ref[...])
    out = vg[:, :d] * jax.nn.sigmoid(vg[:, d:])
    xn = xt_ref[...].reshape(lc, nb, d) + gm_ref[...] * out.reshape(lc, nb, d)
    if batch_major:
        xo_ref[...] = jnp.swapaxes(xn, 0, 1)
    else:
        xo_ref[...] = xn.reshape(rows, d)


def _s5_layer(x, g, sh, sc, gm, s0r, s0i, wts, *, batch_major, lc):
    win, bre, bim, cre, cim, ar, ai, dsk, wglu = wts
    nb = sh.shape[0]
    d = win.shape[0]
    ns = s0r.shape[1]
    if batch_major:
        L = x.shape[1]
        x_spec = pl.BlockSpec((nb, lc, d), lambda c: (0, c, 0))
        x_shape = jax.ShapeDtypeStruct((nb, L, d), F32)
    else:
        L = x.shape[0] // nb
        x_spec = pl.BlockSpec((nb * lc, d), lambda c: (c, 0))
        x_shape = jax.ShapeDtypeStruct((L * nb, d), F32)
    rows = nb * lc
    const = lambda shape: pl.BlockSpec(shape, lambda c: (0,) * len(shape))
    mod3 = lambda t: t.reshape(1, nb, d)
    st_shape = jax.ShapeDtypeStruct((nb, ns), F32)
    kern = functools.partial(_s5_kernel, nb=nb, lc=lc, batch_major=batch_major)
    return pl.pallas_call(
        kern,
        grid=(L // lc,),
        in_specs=[x_spec, const((1, 1, d)), const((1, nb, d)), const((1, nb, d)), const((1, nb, d)),
                  const((nb, ns)), const((nb, ns)), const(win.shape), const(bre.shape), const(bim.shape),
                  const(cre.shape), const(cim.shape), const((1, ns)), const((1, ns)), const((1, d)),
                  const(wglu.shape)],
        out_specs=(x_spec, const((nb, ns)), const((nb, ns))),
        out_shape=(x_shape, st_shape, st_shape),
        scratch_shapes=[pltpu.VMEM((rows, d), F32), pltpu.VMEM((rows, d), F32),
                        pltpu.VMEM((rows, ns), F32), pltpu.VMEM((rows, ns), F32),
                        pltpu.VMEM((rows, d), F32)],
        compiler_params=_cparams(("arbitrary",), 56),
        name="s5",
    )(x, g.reshape(1, 1, d), mod3(sh), mod3(sc), mod3(gm), s0r, s0i, win, bre, bim, cre, cim,
      ar.reshape(1, ns), ai.reshape(1, ns), dsk.reshape(1, d), wglu)


def _router_kernel(x_ref, g_ref, sh_ref, sc_ref, wrt_ref, br_ref, cin_ref, h_ref, ti_ref, tg_ref, rk_ref, cnt_ref):
    @pl.when((pl.program_id(0) == 0) & (pl.program_id(1) == 0))
    def _():
        cnt_ref[...] = cin_ref[...]

    x = x_ref[0]
    h = _norm_mod(x, g_ref[0], sc_ref[0], sh_ref[0])
    hb = h.astype(BF16)
    h_ref[...] = h.reshape(h_ref.shape)
    h_lo = (h - hb.astype(F32)).astype(BF16)
    lt = (_dot_nt(wrt_ref[0], hb) + _dot_nt(wrt_ref[0], h_lo) + _dot_nt(wrt_ref[1], hb)
          + br_ref[...])
    ne = lt.shape[0]
    idx = lax.broadcasted_iota(jnp.int32, lt.shape, 0)
    vals, ids = [], []
    for _ in range(TOP_K):
        m = jnp.max(lt, axis=0, keepdims=True)
        i = jnp.min(jnp.where(lt == m, idx, ne), axis=0, keepdims=True)
        vals.append(m)
        ids.append(i)
        lt = jnp.where(idx == i, -jnp.inf, lt)
    es = [jnp.exp(v - vals[0]) for v in vals]
    tot = es[0] + es[1] + es[2] + es[3]
    rows = lt.shape[1]
    ti_ref[...] = jnp.concatenate(ids, axis=0)
    tg_ref[...] = jnp.concatenate([e / tot for e in es] + [jnp.zeros((SUBLANES - TOP_K, rows), F32)], axis=0)

    upper = jnp.where(lax.broadcasted_iota(jnp.int32, (rows, rows), 0) < lax.broadcasted_iota(jnp.int32, (rows, rows), 1),
                      1.0, 0.0).astype(BF16)
    run = cnt_ref[...]
    ranks = []
    for k in range(TOP_K):
        oh = jnp.where(idx == ids[k], 1.0, 0.0)
        pre = _dot(oh.astype(BF16), upper)
        ranks.append(jnp.sum(oh * (pre + run), axis=0, keepdims=True))
        run = run + jnp.sum(oh, axis=1, keepdims=True)
    cnt_ref[...] = run
    rk_ref[...] = jnp.concatenate(ranks, axis=0).astype(jnp.int32)


def _router(x3, g, sh3, sc3, wrt, br, cnt_in, *, tl):
    nbx, L, d = x3.shape
    lm = sh3.shape[1]
    ne = wrt.shape[1]
    nt = L // tl
    t = nbx * L
    mod_spec = (pl.BlockSpec((1, 1, d), lambda b, c: (b, 0, 0)) if lm == 1
                else pl.BlockSpec((1, tl, d), lambda b, c: (b, c, 0)))
    tok = lambda rows: pl.BlockSpec((rows, tl), lambda b, c: (0, b * nt + c))
    return pl.pallas_call(
        _router_kernel,
        grid=(nbx, nt),
        in_specs=[pl.BlockSpec((1, tl, d), lambda b, c: (b, c, 0)),
                  pl.BlockSpec((1, 1, d), lambda b, c: (0, 0, 0)), mod_spec, mod_spec,
                  pl.BlockSpec((2, ne, d), lambda b, c: (0, 0, 0)), pl.BlockSpec((ne, 1), lambda b, c: (0, 0)),
                  pl.BlockSpec((ne, 1), lambda b, c: (0, 0))],
        out_specs=(pl.BlockSpec((tl, d // LANES, LANES), lambda b, c: (b * nt + c, 0, 0)), tok(TOP_K), tok(SUBLANES),
                   tok(TOP_K), pl.BlockSpec((ne, 1), lambda b, c: (0, 0))),
        out_shape=(jax.ShapeDtypeStruct((t, d // LANES, LANES), F32), jax.ShapeDtypeStruct((TOP_K, t), jnp.int32),
                   jax.ShapeDtypeStruct((SUBLANES, t), F32), jax.ShapeDtypeStruct((TOP_K, t), jnp.int32),
                   jax.ShapeDtypeStruct((ne, 1), F32)),
        compiler_params=_cparams(("arbitrary", "arbitrary"), 32),
        name="router",
    )(x3, g.reshape(1, 1, d), sh3, sc3, wrt, br, cnt_in)


def _pos_kernel(cnt_ref, ti_ref, rk_ref, pos_ref, *, tm):
    cnt = cnt_ref[...]
    ne = cnt.shape[0]
    tiles = jnp.floor((cnt + (tm - 1)) * (1.0 / tm))
    low = jnp.where(lax.broadcasted_iota(jnp.int32, (ne, ne), 1) < lax.broadcasted_iota(jnp.int32, (ne, ne), 0),
                    1.0, 0.0).astype(BF16)
    start = _dot(low, jnp.broadcast_to(tiles, (ne, LANES)).astype(BF16))[:, 0:1] * tm
    ti = ti_ref[...]
    idx = lax.broadcasted_iota(jnp.int32, (ne, ti.shape[1]), 0)
    sel = [jnp.sum(jnp.where(idx == ti[k:k + 1, :], start, 0.0), axis=0, keepdims=True) for k in range(TOP_K)]
    pos_ref[0] = jnp.concatenate(sel, axis=0).astype(jnp.int32) + rk_ref[...]


def _pos(cnt, ti, rk, *, tm, tl):
    k, t = ti.shape
    ne = cnt.shape[0]
    tok = pl.BlockSpec((k, tl), lambda i: (0, i))
    return pl.pallas_call(
        functools.partial(_pos_kernel, tm=tm),
        grid=(t // tl,),
        in_specs=[pl.BlockSpec((ne, 1), lambda i: (0, 0)), tok, tok],
        out_specs=pl.BlockSpec((1, k, tl), lambda i: (i, 0, 0)),
        out_shape=jax.ShapeDtypeStruct((t // tl, k, tl), jnp.int32),
        compiler_params=_cparams(("arbitrary",), 32),
        name="moe_pos",
    )(cnt, ti, rk).reshape(-1)


def _plan_kernel(cnt_ref, te_ref, tv_ref, tf_ref, *, tm, ne, n_tiles):
    shift = tm.bit_length() - 1

    def body(i, carry):
        e, end = carry

        def adv(c):
            e2 = c[0] + 1
            return e2, c[1] + ((cnt_ref[e2] + (tm - 1)) >> shift)

        e2, end2 = lax.while_loop(lambda c: (c[1] <= i) & (c[0] < ne - 1), adv, (e, end))
        te_ref[i] = e2
        tv_ref[i] = (i < end2).astype(jnp.int32)
        tf_ref[i] = ((e2 != e) | (i == 0)).astype(jnp.int32)
        return e2, end2

    lax.fori_loop(0, n_tiles, body, (jnp.int32(0), (cnt_ref[0] + (tm - 1)) >> shift))


def _plan(cnt_i32, *, tm, n_tiles):
    ne = cnt_i32.shape[0]
    smem = pl.BlockSpec(memory_space=pltpu.SMEM)
    out = jax.ShapeDtypeStruct((n_tiles,), jnp.int32)
    return pl.pallas_call(
        functools.partial(_plan_kernel, tm=tm, ne=ne, n_tiles=n_tiles),
        in_specs=[smem], out_specs=(smem, smem, smem), out_shape=(out, out, out), name="moe_plan",
    )(cnt_i32)


def _dispatch_kernel(pos_ref, h_ref, xs_in, xs_out, sem, *, tl):
    del xs_in

    def body(j, c):
        for u in range(SUBLANES):
            r = j * SUBLANES + u
            for k in range(TOP_K):
                pltpu.make_async_copy(h_ref.at[r], xs_out.at[pos_ref[k * tl + r]], sem).start(priority=k % 2)
        return c

    lax.fori_loop(0, tl // SUBLANES, body, 0)
    for k in range(TOP_K):
        pltpu.make_async_copy(h_ref, xs_out.at[pl.ds(0, tl)], sem).wait()


def _dispatch(pos, h, xs, *, tl):
    t, nj, _ = h.shape
    return pl.pallas_call(
        functools.partial(_dispatch_kernel, tl=tl),
        grid=(t // tl,),
        in_specs=[pl.BlockSpec((TOP_K * tl,), lambda i: (i,), memory_space=pltpu.SMEM),
                  pl.BlockSpec((tl, nj, LANES), lambda i: (i, 0, 0)), pl.BlockSpec(memory_space=pl.ANY)],
        out_specs=pl.BlockSpec(memory_space=pl.ANY),
        out_shape=jax.ShapeDtypeStruct(xs.shape, xs.dtype),
        scratch_shapes=[pltpu.SemaphoreType.DMA],
        input_output_aliases={2: 0},
        compiler_params=_cparams(("arbitrary",), 32),
        name="moe_dispatch",
    )(pos, h, xs)


def _combine_kernel(pos_ref, posn_ref, x_ref, gf_ref, tg_ref, fg_ref, outs_hbm, xo_ref, buf, sem, *,
                    tl, nsteps, final):
    i = pl.program_id(0)
    slot = i % 2
    d = x_ref.shape[2]

    def issue(pref, s):
        def body(j, c):
            for u in range(SUBLANES):
                r = j * SUBLANES + u
                for k in range(TOP_K):
                    pltpu.make_async_copy(outs_hbm.at[pref[k * tl + r]], buf.at[s, k, r], sem.at[s]).start(
                        priority=k % 2)
            return c

        lax.fori_loop(0, tl // SUBLANES, body, 0)

    @pl.when(i == 0)
    def _():
        issue(pos_ref, 0)

    @pl.when(i + 1 < nsteps)
    def _():
        issue(posn_ref, 1 - slot)

    gates = jnp.concatenate([tg_ref[...], jnp.zeros((LANES - SUBLANES, tl), F32)], axis=0).T
    for k in range(TOP_K):
        pltpu.make_async_copy(outs_hbm.at[pl.ds(0, tl)], buf.at[slot, k], sem.at[slot]).wait()
    y = gates[:, 0:1] * buf[slot, 0].reshape(tl, d)
    for k in range(1, TOP_K):
        y = y + gates[:, k:k + 1] * buf[slot, k].reshape(tl, d)
    xn = x_ref[0] + gf_ref[0] * y
    if final:
        xn = xn * lax.rsqrt(jnp.mean(xn * xn, axis=-1, keepdims=True) + EPS) * fg_ref[...]
    xo_ref[0] = xn


def _combine(pos, x3, gf3, tg, fgain, outs, *, tl, final):
    nbx, L, d = x3.shape
    nt = L // tl
    nsteps = nbx * nt
    nj = d // LANES
    lm = gf3.shape[1]
    mod_spec = (pl.BlockSpec((1, 1, d), lambda i: (i // nt, 0, 0)) if lm == 1
                else pl.BlockSpec((1, tl, d), lambda i: (i // nt, i % nt, 0)))
    return pl.pallas_call(
        functools.partial(_combine_kernel, tl=tl, nsteps=nsteps, final=final),
        grid=(nsteps,),
        in_specs=[pl.BlockSpec((TOP_K * tl,), lambda i: (i,), memory_space=pltpu.SMEM),
                  pl.BlockSpec((TOP_K * tl,), lambda i: (jnp.minimum(i + 1, nsteps - 1),), memory_space=pltpu.SMEM),
                  pl.BlockSpec((1, tl, d), lambda i: (i // nt, i % nt, 0)), mod_spec,
                  pl.BlockSpec((SUBLANES, tl), lambda i: (0, i)),
                  pl.BlockSpec((1, d), lambda i: (0, 0)), pl.BlockSpec(memory_space=pl.ANY)],
        out_specs=pl.BlockSpec((1, tl, d), lambda i: (i // nt, i % nt, 0)),
        out_shape=jax.ShapeDtypeStruct((nbx, L, d), F32),
        scratch_shapes=[pltpu.VMEM((2, TOP_K, tl, nj, LANES), F32), pltpu.SemaphoreType.DMA((2,))],
        compiler_params=_cparams(("arbitrary",), 48),
        name="moe_combine",
    )(pos, pos, x3, gf3, tg, fgain.reshape(1, d), outs)


def _gmm_kernel(te_ref, tv_ref, tf_ref, x_ref, wgu_ref, bgu_ref, wd_ref, bd_ref, o_ref,
                wgu_s, wd_s, act_s):
    i = pl.program_id(0)
    de = wd_s.shape[0]
    nch = 4
    cw = de // nch

    @pl.when(tf_ref[i] == 1)
    def _():
        for n in range(2 * nch):
            wgu_s[:, n * cw:(n + 1) * cw] = wgu_ref[0, :, n * cw:(n + 1) * cw].astype(BF16)
        for n in range(nch):
            wd_s[n * cw:(n + 1) * cw, :] = wd_ref[0, n * cw:(n + 1) * cw, :].astype(BF16)

    @pl.when(tv_ref[i] == 1)
    def _():
        x = x_ref[...].reshape(act_s.shape[0], wgu_s.shape[0]).astype(BF16)
        for n in range(nch):
            glu = _dot(x, wgu_s[:, n * cw:(n + 1) * cw]) + bgu_ref[0, :, n * cw:(n + 1) * cw]
            lin = _dot(x, wgu_s[:, de + n * cw:de + (n + 1) * cw]) + bgu_ref[0, :, de + n * cw:de + (n + 1) * cw]
            glu = jnp.minimum(glu, SWIGLU_LIMIT)
            lin = jnp.clip(lin, -SWIGLU_LIMIT, SWIGLU_LIMIT)
            act = glu * jax.nn.sigmoid(SWIGLU_ALPHA * glu) * (lin + 1.0)
            act_s[:, n * cw:(n + 1) * cw] = act.astype(BF16)
        o_ref[...] = (_dot(act_s[...], wd_s[...]) + bd_ref[0]).reshape(o_ref.shape)

    @pl.when(tv_ref[i] == 0)
    def _():
        o_ref[...] = jnp.zeros_like(o_ref)


def _gmm(xs, tile_expert, tile_valid, tile_first, w_gu, b_gu, w_down, b_down, *, tm, layer):
    n_slots, nj, _ = xs.shape
    depth, ne, d, de2 = w_gu.shape
    de = de2 // 2
    n_tiles = n_slots // tm
    b_gu = b_gu.reshape(depth * ne, 1, de2)
    b_down = b_down.reshape(depth * ne, 1, d)
    rows = pl.BlockSpec((tm, nj, LANES), lambda i, te, tv, tf: (i, 0, 0))
    grid_spec = pltpu.PrefetchScalarGridSpec(
        num_scalar_prefetch=3,
        grid=(n_tiles,),
        in_specs=[rows,
                  pl.BlockSpec((None, 1, d, de2), lambda i, te, tv, tf: (layer, te[i], 0, 0)),
                  pl.BlockSpec((1, 1, de2), lambda i, te, tv, tf: (layer * ne + te[i], 0, 0)),
                  pl.BlockSpec((None, 1, de, d), lambda i, te, tv, tf: (layer, te[i], 0, 0)),
                  pl.BlockSpec((1, 1, d), lambda i, te, tv, tf: (layer * ne + te[i], 0, 0))],
        out_specs=rows,
        scratch_shapes=[pltpu.VMEM((d, de2), BF16), pltpu.VMEM((de, d), BF16), pltpu.VMEM((tm, de), BF16)],
    )
    return pl.pallas_call(
        _gmm_kernel,
        grid_spec=grid_spec,
        out_shape=jax.ShapeDtypeStruct((n_slots, nj, LANES), F32),
        compiler_params=_cparams(("arbitrary",), 58),
        name="gmm",
    )(tile_expert, tile_valid, tile_first, xs, w_gu, b_gu, w_down, b_down)


def _moe_layer(xp, xs_, g, mods_p, mods_s, wr, br_, w_gu, b_gu, w_down, b_down, slots, fgain, *, layer, final):
    bp, seq, d = xp.shape
    bs = xs_.shape[0]
    ne = wr.shape[1]
    tm = GMM_ROWS
    n_tiles = -(-(TOP_K * (bp * seq + bs)) // tm) + ne
    sh_p, sc_p, gf_p = mods_p
    sh_s, sc_s, gf_s = mods_s
    wr_t = wr.T
    wr_hi = wr_t.astype(BF16)
    wrt = jnp.stack([wr_hi, (wr_t - wr_hi.astype(F32)).astype(BF16)])
    br = br_.reshape(ne, 1)
    cnt0 = jnp.zeros((ne, 1), F32)
    hp, tip, tgp, rkp, cnt1 = _router(xp, g, sh_p[:, None], sc_p[:, None], wrt, br, cnt0, tl=ROUTER_ROWS)
    hs, tis, tgs, rks, cnt = _router(xs_[None], g, sh_s[None], sc_s[None], wrt, br, cnt1, tl=bs)
    pos_p = _pos(cnt, tip, rkp, tm=tm, tl=ROUTER_ROWS)
    pos_s = _pos(cnt, tis, rks, tm=tm, tl=bs)
    te, tv, tf = _plan(cnt.reshape(ne).astype(jnp.int32), tm=tm, n_tiles=n_tiles)
    if slots is None:
        slots = jnp.zeros((n_tiles * tm, d // LANES, LANES), F32)
    slots = _dispatch(pos_p, hp, slots, tl=ROUTER_ROWS)
    slots = _dispatch(pos_s, hs, slots, tl=bs)
    outs = _gmm(slots, te, tv, tf, w_gu, b_gu, w_down, b_down, tm=tm, layer=layer)
    xp = _combine(pos_p, xp, gf_p[:, None], tgp, fgain, outs, tl=ROUTER_ROWS, final=final)
    xs_ = _combine(pos_s, xs_[None], gf_s[None], tgs, fgain, outs, tl=bs, final=final)[0]
    return xp, xs_, slots


def _mlstm_kernel(x_ref, g_ref, sh_ref, sc_ref, gm_ref, wup_ref, cw_ref, cb_ref, wq_ref, wk_ref, wv_ref,
                  wg_ref, bg_ref, lnw_ref, skip_ref, wdn_ref,
                  xo_ref, cst_ref, nst_ref, mst_ref, conv_ref,
                  xm_s, up_s, xc_s, q_s, k_s, v_s, *, lc, heads, dh):
    c = pl.program_id(1)
    inner = heads * dh
    pad = SUBLANES

    @pl.when(c == 0)
    def _():
        cst_ref[...] = jnp.zeros_like(cst_ref)
        nst_ref[...] = jnp.zeros_like(nst_ref)
        mst_ref[...] = jnp.zeros_like(mst_ref)
        xm_s[0:pad, :] = jnp.zeros((pad, inner), F32)

    x = x_ref[0]
    h = _norm_mod(x, g_ref[0], sc_ref[0], sh_ref[0]).astype(BF16)
    up_s[...] = _dot(h, wup_ref[...])
    xm_s[pad:pad + lc, :] = up_s[:, :inner]
    conv = cb_ref[...]
    for k in range(ML_CONV):
        off = pad - (ML_CONV - 1) + k
        conv = conv + cw_ref[k:k + 1, :] * xm_s[off:off + lc, :]
    xc_s[...] = _silu(conv)
    xm_s[0:pad, :] = xm_s[lc:lc + pad, :]
    conv_ref[0] = xm_s[0:pad, :]

    gacc = jnp.broadcast_to(bg_ref[...], (lc, LANES))
    for hd in range(heads):
        hs = slice(hd * dh, (hd + 1) * dh)
        xch = xc_s[:, hs].astype(BF16)
        q = _dot(xch, wq_ref[hd])
        k = _dot(xch, wk_ref[hd]) * (dh ** -0.5)
        v = _dot(up_s[:, hs].astype(BF16), wv_ref[hd])
        qb = q.astype(BF16)
        vb = v.astype(BF16)
        q_s[:, hs] = qb
        k_s[:, hs] = k
        v_s[:, hs] = vb
        gacc = gacc + _dot(qb, wg_ref[0, hd]) + _dot(k.astype(BF16), wg_ref[1, hd]) + _dot(vb, wg_ref[2, hd])

    lf = _log_sigmoid(gacc)
    row = lax.broadcasted_iota(jnp.int32, (lc, lc), 0)
    col = lax.broadcasted_iota(jnp.int32, (lc, lc), 1)
    causal = row >= col
    tri = jnp.where(causal, 1.0, 0.0).astype(BF16)
    hi = lf.astype(BF16)
    r1 = lf - hi.astype(F32)
    mid = r1.astype(BF16)
    lo = (r1 - mid.astype(F32)).astype(BF16)
    bcum = _dot(tri, hi) + _dot(tri, mid) + _dot(tri, lo)
    g_t = gacc.T
    b_t = bcum.T

    acc = jnp.zeros((lc, xo_ref.shape[2]), F32)
    for hd in range(heads):
        hs = slice(hd * dh, (hd + 1) * dh)
        ig_c = gacc[:, hd:hd + 1]
        b_c = bcum[:, heads + hd:heads + hd + 1]
        ig_r = g_t[hd:hd + 1, :]
        b_r = b_t[heads + hd:heads + hd + 1, :]
        m_prev = mst_ref[0, hd][:, 0:1]
        dm = jnp.where(causal, (b_c - b_r) + ig_r, -jnp.inf)
        inter = b_c + m_prev
        m_t = jnp.maximum(inter, jnp.max(dm, axis=1, keepdims=True))
        w_intra = jnp.exp(dm - m_t)
        w_inter = jnp.exp(inter - m_t)
        qb = q_s[:, hs]
        kf = k_s[:, hs]
        vb = v_s[:, hs]
        cmat = cst_ref[0, hd]
        nvec = nst_ref[0, hd]
        qk = _dot_nt(qb, kf.astype(BF16)) * w_intra
        num = _dot(qk.astype(BF16), vb) + w_inter * _dot(qb, cmat.astype(BF16))
        qn = jnp.sum(qb.astype(F32) * nvec.astype(BF16).astype(F32), axis=1, keepdims=True)
        den = jnp.sum(qk, axis=1, keepdims=True) + w_inter * qn
        hh = num / jnp.maximum(jnp.abs(den), jnp.exp(-m_t))

        bl = b_c[lc - 1:lc, :]
        m_new = m_t[lc - 1:lc, :]
        w_s = jnp.exp(bl - b_c + ig_c - m_new)
        decay = jnp.exp(bl + m_prev - m_new)
        kw = w_s * kf
        cst_ref[0, hd] = decay * cmat + _dot(kw.T.astype(BF16), vb)
        nst_ref[0, hd] = decay * nvec + jnp.sum(kw, axis=0, keepdims=True)
        mst_ref[0, hd] = jnp.broadcast_to(m_new, (1, LANES))

        mu = jnp.mean(hh, axis=1, keepdims=True)
        hc = hh - mu
        var = jnp.mean(hc * hc, axis=1, keepdims=True)
        hn = hc * lax.rsqrt(var + EPS) * lnw_ref[:, hs]
        o = (hn + skip_ref[:, hs] * xc_s[:, hs]) * _silu(up_s[:, inner + hd * dh:inner + (hd + 1) * dh])
        acc = acc + _dot(o.astype(BF16), wdn_ref[hs, :])
    xo_ref[0] = x + gm_ref[0] * acc


def _mlstm_prompt(x, g, sh, sc, gm, wts, *, lc):
    wup, cw, cb, wq, wk, wv, wg, bg, lnw, skip, wdn = wts
    nb, L, d = x.shape
    heads, dh, _ = wq.shape
    inner = heads * dh
    const = lambda shape: pl.BlockSpec(shape, lambda b, c: (0,) * len(shape), pipeline_mode=pl.Buffered(1))
    per_b = lambda shape: pl.BlockSpec(shape, lambda b, c: (b,) + (0,) * (len(shape) - 1))
    mod3 = lambda t: t.reshape(nb, 1, d)
    kern = functools.partial(_mlstm_kernel, lc=lc, heads=heads, dh=dh)
    return pl.pallas_call(
        kern,
        grid=(nb, L // lc),
        in_specs=[pl.BlockSpec((1, lc, d), lambda b, c: (b, c, 0)), const((1, 1, d)),
                  per_b((1, 1, d)), per_b((1, 1, d)), per_b((1, 1, d)),
                  const(wup.shape), const(cw.shape), const((1, inner)), const(wq.shape), const(wk.shape),
                  const(wv.shape), const(wg.shape), const((1, LANES)), const((1, inner)), const((1, inner)),
                  const(wdn.shape)],
        out_specs=(pl.BlockSpec((1, lc, d), lambda b, c: (b, c, 0)),
                   per_b((1, heads, dh, dh)), per_b((1, heads, 1, dh)), per_b((1, heads, 1, LANES)),
                   per_b((1, SUBLANES, inner))),
        out_shape=(jax.ShapeDtypeStruct((nb, L, d), F32),
                   jax.ShapeDtypeStruct((nb, heads, dh, dh), F32),
                   jax.ShapeDtypeStruct((nb, heads, 1, dh), F32),
                   jax.ShapeDtypeStruct((nb, heads, 1, LANES), F32),
                   jax.ShapeDtypeStruct((nb, SUBLANES, inner), F32)),
        scratch_shapes=[pltpu.VMEM((lc + SUBLANES, inner), F32), pltpu.VMEM((lc, 2 * inner), F32),
                        pltpu.VMEM((lc, inner), F32), pltpu.VMEM((lc, inner), BF16),
                        pltpu.VMEM((lc, inner), F32), pltpu.VMEM((lc, inner), BF16)],
        compiler_params=_cparams(("arbitrary", "arbitrary"), 58),
        name="mlstm",
    )(x, g.reshape(1, 1, d), mod3(sh), mod3(sc), mod3(gm), wup, cw, cb.reshape(1, inner), wq, wk, wv, wg, bg,
      lnw.reshape(1, inner), skip.reshape(1, inner), wdn)


def _mls_proj_kernel(x_ref, g_ref, sh_ref, sc_ref, wxm_ref, wz_ref, buf_ref, cw_ref, cb_ref, wq_ref, wk_ref,
                     wv_ref, wg_ref, bg_ref,
                     q_ref, k_ref, v_ref, xc_ref, z_ref, gate_ref, nbuf_ref):
    hd = pl.program_id(0)

    @pl.when(hd == 0)
    def _():
        gate_ref[...] = jnp.broadcast_to(bg_ref[...], gate_ref.shape)

    h = _norm_mod(x_ref[...], g_ref[...], sc_ref[...], sh_ref[...]).astype(BF16)
    xm = _dot(h, wxm_ref[...])
    z_ref[...] = _dot(h, wz_ref[...])
    conv = cb_ref[...] + cw_ref[ML_CONV - 1:ML_CONV, :] * xm
    for k in range(ML_CONV - 1):
        conv = conv + cw_ref[k:k + 1, :] * buf_ref[k]
    for k in range(ML_CONV - 2):
        nbuf_ref[k] = buf_ref[k + 1]
    nbuf_ref[ML_CONV - 2] = xm
    xc = _silu(conv)
    xc_ref[...] = xc
    xcb = xc.astype(BF16)
    dh = xm.shape[1]
    q = _dot(xcb, wq_ref[0])
    k = _dot(xcb, wk_ref[0]) * (dh ** -0.5)
    v = _dot(xm.astype(BF16), wv_ref[0])
    q_ref[...] = q
    k_ref[...] = k
    v_ref[...] = v
    gate_ref[...] += (_dot(q.astype(BF16), wg_ref[0, 0]) + _dot(k.astype(BF16), wg_ref[1, 0])
                      + _dot(v.astype(BF16), wg_ref[2, 0]))


def _mls_proj(x, g, sh, sc, buf, wts):
    wup, cw, cb, wq, wk, wv, wg, bg = wts
    nb, d = x.shape
    heads, dh, _ = wq.shape
    inner = heads * dh
    full = lambda shape: pl.BlockSpec(shape, lambda h: (0,) * len(shape))
    colblk = pl.BlockSpec((nb, dh), lambda h: (0, h))
    act = jax.ShapeDtypeStruct((nb, inner), F32)
    return pl.pallas_call(
        _mls_proj_kernel,
        grid=(heads,),
        in_specs=[full((nb, d)), full((1, d)), full((nb, d)), full((nb, d)),
                  pl.BlockSpec((d, dh), lambda h: (0, h)), pl.BlockSpec((d, dh), lambda h: (0, heads + h)),
                  pl.BlockSpec((ML_CONV - 1, nb, dh), lambda h: (0, 0, h)),
                  pl.BlockSpec((ML_CONV, dh), lambda h: (0, h)), pl.BlockSpec((1, dh), lambda h: (0, h)),
                  pl.BlockSpec((1, dh, dh), lambda h: (h, 0, 0)), pl.BlockSpec((1, dh, dh), lambda h: (h, 0, 0)),
                  pl.BlockSpec((1, dh, dh), lambda h: (h, 0, 0)),
                  pl.BlockSpec((3, 1, dh, LANES), lambda h: (0, h, 0, 0)), full((1, LANES))],
        out_specs=(colblk, colblk, colblk, colblk, colblk, full((nb, LANES)),
                   pl.BlockSpec((ML_CONV - 1, nb, dh), lambda h: (0, 0, h))),
        out_shape=(act, act, act, act, act, jax.ShapeDtypeStruct((nb, LANES), F32),
                   jax.ShapeDtypeStruct((ML_CONV - 1, nb, inner), F32)),
        compiler_params=_cparams(("arbitrary",), 32),
        name="mls_proj",
    )(x, g.reshape(1, d), sh, sc, wup, wup, buf, cw, cb.reshape(1, inner), wq, wk, wv, wg, bg)


def _mls_gate_scalars(ig, fg, m0):
    lf = _log_sigmoid(fg)
    inter = lf + m0
    m_t = jnp.maximum(inter, ig)
    return m_t, jnp.exp(ig - m_t), jnp.exp(inter - m_t)


def _mls_state_kernel(q_ref, k_ref, v_ref, ig_ref, fg_ref, m0_ref, c_ref, qc_ref, cn_ref, *, heads, dh):
    _, w_in, w_dec = _mls_gate_scalars(ig_ref[0], fg_ref[0], m0_ref[0])
    rowmask = lax.broadcasted_iota(jnp.int32, (LANES, dh), 0) == 0
    for hd in range(heads):
        hs = slice(hd * dh, (hd + 1) * dh)
        wi = w_in[:, hd:hd + 1]
        wd = w_dec[:, hd:hd + 1]
        cmat = c_ref[0, hd]
        q8 = jnp.broadcast_to(q_ref[0, :, hs], (SUBLANES, dh)).astype(BF16)
        qc_ref[0, :, hs] = _dot(q8, cmat.astype(BF16))[0:1, :]
        kw = jnp.where(rowmask, wi * k_ref[0, :, hs], 0.0)
        vv = jnp.where(rowmask, v_ref[0, :, hs], 0.0)
        cn_ref[0, hd] = wd * cmat + _dot(kw.T.astype(BF16), vv.astype(BF16))


def _mls_state(q, k, v, ig, fg, m0, cst):
    nb, inner = q.shape
    _, heads, dh, _ = cst.shape
    row = lambda w: pl.BlockSpec((1, 1, w), lambda b: (b, 0, 0))
    r3 = lambda t: t.reshape(nb, 1, t.shape[1])
    kern = functools.partial(_mls_state_kernel, heads=heads, dh=dh)
    qc, cn = pl.pallas_call(
        kern,
        grid=(nb,),
        in_specs=[row(inner), row(inner), row(inner), row(LANES), row(LANES), row(LANES),
                  pl.BlockSpec((1, heads, dh, dh), lambda b: (b, 0, 0, 0))],
        out_specs=(row(inner), pl.BlockSpec((1, heads, dh, dh), lambda b: (b, 0, 0, 0))),
        out_shape=(jax.ShapeDtypeStruct((nb, 1, inner), F32), jax.ShapeDtypeStruct(cst.shape, F32)),
        compiler_params=_cparams(("arbitrary",), 40),
        name="mls_state",
    )(r3(q), r3(k), r3(v), r3(ig), r3(fg), r3(m0), cst)
    return qc.reshape(nb, inner), cn


def _mls_post_kernel(x_ref, gm_ref, q_ref, k_ref, v_ref, xc_ref, z_ref, qc_ref, n_ref, ig_ref, fg_ref, m0_ref,
                     lnw_ref, skip_ref, wdn_ref, xo_ref, nn_ref, mn_ref, *, heads, dh):
    m_t, w_in, w_dec = _mls_gate_scalars(ig_ref[...], fg_ref[...], m0_ref[...])
    mn_ref[...] = m_t
    acc = jnp.zeros(xo_ref.shape, F32)
    for hd in range(heads):
        hs = slice(hd * dh, (hd + 1) * dh)
        wi = w_in[:, hd:hd + 1]
        wd = w_dec[:, hd:hd + 1]
        mt = m_t[:, hd:hd + 1]
        q = q_ref[:, hs]
        k = k_ref[:, hs]
        qr = q.astype(BF16).astype(F32)
        qk = jnp.sum(qr * k.astype(BF16).astype(F32), axis=1, keepdims=True) * wi
        nvec = n_ref[:, hs]
        num = qk.astype(BF16).astype(F32) * v_ref[:, hs].astype(BF16).astype(F32) + wd * qc_ref[:, hs]
        den = qk + wd * jnp.sum(qr * nvec.astype(BF16).astype(F32), axis=1, keepdims=True)
        hh = num / jnp.maximum(jnp.abs(den), jnp.exp(-mt))
        nn_ref[:, hs] = wd * nvec + wi * k
        mu = jnp.mean(hh, axis=1, keepdims=True)
        hc = hh - mu
        var = jnp.mean(hc * hc, axis=1, keepdims=True)
        hn = hc * lax.rsqrt(var + EPS) * lnw_ref[:, hs]
        o = (hn + skip_ref[:, hs] * xc_ref[:, hs]) * _silu(z_ref[:, hs])
        acc = acc + _dot(o.astype(BF16), wdn_ref[hs, :])
    xo_ref[...] = x_ref[...] + gm_ref[...] * acc


def _mls_post(x, gm, q, k, v, xc, z, qc, n0, ig, fg, m0, lnw, skip, wdn, *, heads, dh):
    nb, d = x.shape
    inner = heads * dh
    kern = functools.partial(_mls_post_kernel, heads=heads, dh=dh)
    return pl.pallas_call(
        kern,
        out_shape=(jax.ShapeDtypeStruct((nb, d), F32), jax.ShapeDtypeStruct((nb, inner), F32),
                   jax.ShapeDtypeStruct((nb, LANES), F32)),
        compiler_params=pltpu.CompilerParams(vmem_limit_bytes=40 * MIB),
        name="mls_post",
    )(x, gm, q, k, v, xc, z, qc, n0, ig, fg, m0, lnw.reshape(1, inner), skip.reshape(1, inner), wdn)


def _pad_lanes(t):
    return jnp.pad(t, ((0, 0), (0, LANES - t.shape[1])))


def kernel(x_prompt, x_sample, c_prompt, c_sample, state_s5_re, state_s5_im, state_mlstm_C, state_mlstm_n, state_mlstm_m, state_mlstm_conv, norm_mix_g, norm_ffn_g, final_norm_g, ada_w, ada_b, s5_w_in, s5_lam_re, s5_lam_im, s5_log_dt, s5_b_re, s5_b_im, s5_c_re, s5_c_im, s5_d, s5_w_glu, ml_w_up, ml_conv_w, ml_conv_b, ml_w_q, ml_w_k, ml_w_v, ml_w_gate, ml_b_gate, ml_ln_w, ml_skip, ml_w_down, moe_w_router, moe_b_router, moe_w_gu, moe_b_gu, moe_w_down, moe_b_down):
    bp, seq, d = x_prompt.shape
    bs = x_sample.shape[0]
    tp = bp * seq
    _, groups, pstate, gch = s5_b_re.shape
    ns = groups * pstate
    heads, dh = ml_w_q.shape[1], ml_w_q.shape[2]
    inner = heads * dh
    ne = moe_w_router.shape[2]

    mods = _ada(jnp.concatenate([c_prompt, c_sample], axis=0), ada_w, ada_b)

    def mod(i, j):
        m = mods[i, :, j * d:(j + 1) * d]
        return m[:bp], m[bp:]

    ar, ai, bbr, bbi = _s5_prep(s5_lam_re[0], s5_lam_im[0], s5_log_dt[0], s5_b_re[0], s5_b_im[0])
    nblk = d // LANES
    bre = _block_diag(bbr.reshape(groups, gch, pstate), nblk).astype(BF16)
    bim = _block_diag(bbi.reshape(groups, gch, pstate), nblk).astype(BF16)
    cre = _block_diag(s5_c_re[0].transpose(0, 2, 1), nblk).astype(BF16)
    cim = _block_diag(s5_c_im[0].transpose(0, 2, 1), nblk).astype(BF16)
    s5_wts = (s5_w_in[0].astype(BF16), bre, bim, cre, cim, ar.reshape(ns), ai.reshape(ns), s5_d[0],
              s5_w_glu[0].astype(BF16))
    (sh_p, sh_s), (sc_p, sc_s), (gm_p, gm_s) = mod(0, 0), mod(0, 1), mod(0, 2)
    zeros_st = jnp.zeros((bp, ns), F32)
    xp, p_re, p_im = _s5_layer(x_prompt, norm_mix_g[0], sh_p, sc_p, gm_p, zeros_st, zeros_st, s5_wts,
                               batch_major=True, lc=S5_CHUNK)
    xs_, s_re, s_im = _s5_layer(x_sample.reshape(bs, d), norm_mix_g[0], sh_s, sc_s, gm_s,
                                state_s5_re[0].reshape(bs, ns), state_s5_im[0].reshape(bs, ns), s5_wts,
                                batch_major=False, lc=1)

    def moe_layer(i, xp, xs_, slots, final):
        (sh_p, sh_s), (sc_p, sc_s), (gf_p, gf_s) = mod(i, 3), mod(i, 4), mod(i, 5)
        return _moe_layer(xp, xs_, norm_ffn_g[i], (sh_p, sc_p, gf_p), (sh_s, sc_s, gf_s), moe_w_router[i],
                          moe_b_router[i], moe_w_gu, moe_b_gu, moe_w_down, moe_b_down, slots,
                          final_norm_g, layer=i, final=final)

    xp, xs_, slots = moe_layer(0, xp, xs_, None, False)

    wg = _pad_lanes(ml_w_gate[0]).reshape(3, heads, dh, LANES).astype(BF16)
    bg = _pad_lanes(ml_b_gate[0][None])
    wup = ml_w_up[0].astype(BF16)
    wq, wk, wv = ml_w_q[0].astype(BF16), ml_w_k[0].astype(BF16), ml_w_v[0].astype(BF16)
    wdn = ml_w_down[0].astype(BF16)
    (sh_p, sh_s), (sc_p, sc_s), (gm_p, gm_s) = mod(1, 0), mod(1, 1), mod(1, 2)
    xp, p_c, p_n, p_m, p_tail = _mlstm_prompt(
        xp, norm_mix_g[1], sh_p, sc_p, gm_p,
        (wup, ml_conv_w[0], ml_conv_b[0], wq, wk, wv, wg, bg, ml_ln_w[0], ml_skip[0], wdn), lc=ML_CHUNK)
    p_n = p_n.reshape(bp, heads, dh)
    p_m = p_m[:, :, 0, 0]
    p_conv = p_tail[:, SUBLANES - (ML_CONV - 1):, :]

    buf = state_mlstm_conv[0].transpose(1, 0, 2)
    q, k, v, xc, z, gates, nbuf = _mls_proj(xs_, norm_mix_g[1], sh_s, sc_s, buf,
                                            (wup, ml_conv_w[0], ml_conv_b[0], wq, wk, wv, wg, bg))
    ig = _pad_lanes(gates[:, :heads])
    fg = _pad_lanes(gates[:, heads:2 * heads])
    m0 = _pad_lanes(state_mlstm_m[0])
    qc, s_c = _mls_state(q, k, v, ig, fg, m0, state_mlstm_C[0])
    xs_, s_n, s_m = _mls_post(xs_, gm_s, q, k, v, xc, z, qc, state_mlstm_n[0].reshape(bs, inner), ig, fg, m0,
                              ml_ln_w[0], ml_skip[0], wdn, heads=heads, dh=dh)
    s_n = s_n.reshape(bs, heads, dh)
    s_m = s_m[:, :heads]
    s_conv = nbuf.transpose(1, 0, 2)

    y_p, y_s, _ = moe_layer(1, xp, xs_, slots, True)
    y_s = y_s.reshape(bs, 1, d)
    return (y_p, y_s,
            p_re.reshape(1, bp, groups, pstate), p_im.reshape(1, bp, groups, pstate),
            p_c[None], p_n[None], p_m[None], p_conv[None],
            s_re.reshape(1, bs, groups, pstate), s_im.reshape(1, bs, groups, pstate),
            s_c[None], s_n[None], s_m[None], s_conv[None])
```

```python
import functools

import jax
import jax.numpy as jnp
from jax import lax
from jax.experimental import pallas as pl
from jax.experimental.pallas import tpu as pltpu

F32 = jnp.float32
BF16 = jnp.bfloat16
EPS = 1e-6
TOP_K = 4
SWIGLU_LIMIT = 7.0
SWIGLU_ALPHA = 1.702
ML_CONV = 4

LANES = 128
SUBLANES = 8
MIB = 1024 * 1024

S5_CHUNK = 64
ML_CHUNK = 256
ROUTER_ROWS = 512
GMM_ROWS = 512
NORM_ROWS = 512


def _cparams(semantics, vmem_mib):
    return pltpu.CompilerParams(dimension_semantics=semantics, vmem_limit_bytes=int(vmem_mib * MIB))


def _dot(a, b):
    return jnp.dot(a, b, preferred_element_type=F32)


def _dot_nt(a, b):
    return lax.dot_general(a, b, (((1,), (1,)), ((), ())), preferred_element_type=F32)


def _norm_mod(x, g, sc, sh):
    y = x * lax.rsqrt(jnp.mean(x * x, axis=-1, keepdims=True) + EPS)
    return (y * g) * (1.0 + sc) + sh


def _silu(x):
    return x * jax.nn.sigmoid(x)


def _log_sigmoid(x):
    return -(jnp.maximum(-x, 0.0) + jnp.log1p(jnp.exp(-jnp.abs(x))))


def _ada_kernel(c_ref, w_ref, b_ref, o_ref):
    c = c_ref[...]
    o_ref[0] = _dot(_silu(c).astype(BF16), w_ref[0].astype(BF16)) + b_ref[0]


def _ada(c_all, ada_w, ada_b):
    depth, d, n = ada_w.shape
    rows = c_all.shape[0]
    tn = n // 4
    return pl.pallas_call(
        _ada_kernel,
        grid=(depth, n // tn),
        in_specs=[pl.BlockSpec((rows, d), lambda i, j: (0, 0)),
                  pl.BlockSpec((1, d, tn), lambda i, j: (i, 0, j)),
                  pl.BlockSpec((1, 1, tn), lambda i, j: (i, 0, j))],
        out_specs=pl.BlockSpec((1, rows, tn), lambda i, j: (i, 0, j)),
        out_shape=jax.ShapeDtypeStruct((depth, rows, n), F32),
        compiler_params=_cparams(("arbitrary", "arbitrary"), 32),
        name="ada",
    )(c_all, ada_w, ada_b.reshape(depth, 1, n))


def _s5_prep_kernel(lr_ref, li_ref, ldt_ref, br_ref, bi_ref, ar_ref, ai_ref, bbr_ref, bbi_ref):
    dt = jnp.exp(ldt_ref[...])
    lr = lr_ref[...]
    li = li_ref[...]
    mag = jnp.exp(lr * dt)
    ar = mag * jnp.cos(li * dt)
    ai = mag * jnp.sin(li * dt)
    den = lr * lr + li * li
    nr = ar - 1.0
    wr = (nr * lr + ai * li) / den
    wi = (ai * lr - nr * li) / den
    br = br_ref[...]
    bi = bi_ref[...]
    ar_ref[...] = ar
    ai_ref[...] = ai
    bbr_ref[...] = wr * br - wi * bi
    bbi_ref[...] = wr * bi + wi * br


def _s5_prep(lam_re, lam_im, log_dt, b_re, b_im):
    g, p, c = b_re.shape
    rep = lambda t: jnp.repeat(t, c, axis=0)
    bt = lambda t: t.transpose(0, 2, 1).reshape(g * c, p)
    shp = jax.ShapeDtypeStruct((g * c, p), F32)
    ar, ai, bbr, bbi = pl.pallas_call(
        _s5_prep_kernel, out_shape=(shp, shp, shp, shp), name="s5_prep",
    )(rep(lam_re), rep(lam_im), rep(log_dt[:, None]), bt(b_re), bt(b_im))
    return ar[::c], ai[::c], bbr, bbi


def _block_diag(t, nblk):
    g, a, b = t.shape
    gl = g // nblk
    t4 = t.reshape(nblk, gl, a, b)
    eye = jnp.eye(gl, dtype=t.dtype)
    return jnp.einsum('jgab,gh->jgahb', t4, eye).reshape(nblk, gl * a, gl * b)


def _s5_kernel(x_ref, g_ref, sh_ref, sc_ref, gm_ref, s0r_ref, s0i_ref, win_ref, bre_ref, bim_ref,
               cre_ref, cim_ref, ar_ref, ai_ref, d_ref, wglu_ref,
               xo_ref, fr_ref, fi_ref,
               xt_ref, u_ref, sre_ref, sim_ref, y_ref, *, nb, lc, batch_major):
    c = pl.program_id(0)
    rows = nb * lc
    d = u_ref.shape[1]
    ns = sre_ref.shape[1]
    nblk = bre_ref.shape[0]
    kb = d // nblk
    sb = ns // nblk

    @pl.when(c == 0)
    def _():
        fr_ref[...] = s0r_ref[...]
        fi_ref[...] = s0i_ref[...]

    if batch_major:
        xt_ref[...] = jnp.swapaxes(x_ref[...], 0, 1).reshape(rows, d)
    else:
        xt_ref[...] = x_ref[...]
    x3 = xt_ref[...].reshape(lc, nb, d)
    h = _norm_mod(x3, g_ref[...], sc_ref[...], sh_ref[...]).reshape(rows, d).astype(BF16)
    u = _dot(h, win_ref[...])
    u_ref[...] = u
    ub = u.astype(BF16)
    cb = max(LANES, (4 * SUBLANES * LANES) // nb)

    def scan_block(cs):
        a_r = jnp.broadcast_to(ar_ref[:, cs], (nb, cb))
        a_i = jnp.broadcast_to(ai_ref[:, cs], (nb, cb))
        sr, si = fr_ref[:, cs], fi_ref[:, cs]
        for t in range(lc):
            rs = slice(t * nb, (t + 1) * nb)
            nr = a_r * sr - a_i * si + sre_ref[rs, cs]
            ni = a_r * si + a_i * sr + sim_ref[rs, cs]
            sre_ref[rs, cs] = nr
            sim_ref[rs, cs] = ni
            sr, si = nr, ni
        fr_ref[:, cs] = sr
        fi_ref[:, cs] = si

    for j in range(nblk):
        js = slice(j * sb, (j + 1) * sb)
        ks = slice(j * kb, (j + 1) * kb)
        uj = ub[:, ks]
        sre_ref[:, js] = _dot(uj, bre_ref[j])
        sim_ref[:, js] = _dot(uj, bim_ref[j])
        for c in range(sb // cb):
            scan_block(slice(j * sb + c * cb, j * sb + (c + 1) * cb))
        yj = _dot(sre_ref[:, js].astype(BF16), cre_ref[j]) - _dot(sim_ref[:, js].astype(BF16), cim_ref[j])
        y_ref[:, ks] = yj + d_ref[:, ks] * u_ref[:, ks]

    yg = jax.nn.gelu(y_ref[...]).astype(BF16)
    vg = _dot(yg, wglu_ref[...])
    out = vg[:, :d] * jax.nn.sigmoid(vg[:, d:])
    xn = xt_ref[...].reshape(lc, nb, d) + gm_ref[...] * out.reshape(lc, nb, d)
    if batch_major:
        xo_ref[...] = jnp.swapaxes(xn, 0, 1)
    else:
        xo_ref[...] = xn.reshape(rows, d)


def _s5_layer(x, g, sh, sc, gm, s0r, s0i, wts, *, batch_major, lc):
    win, bre, bim, cre, cim, ar, ai, dsk, wglu = wts
    nb = sh.shape[0]
    d = win.shape[0]
    ns = s0r.shape[1]
    if batch_major:
        L = x.shape[1]
        x_spec = pl.BlockSpec((nb, lc, d), lambda c: (0, c, 0))
        x_shape = jax.ShapeDtypeStruct((nb, L, d), F32)
    else:
        L = x.shape[0] // nb
        x_spec = pl.BlockSpec((nb * lc, d), lambda c: (c, 0))
        x_shape = jax.ShapeDtypeStruct((L * nb, d), F32)
    rows = nb * lc
    const = lambda shape: pl.BlockSpec(shape, lambda c: (0,) * len(shape))
    mod3 = lambda t: t.reshape(1, nb, d)
    st_shape = jax.ShapeDtypeStruct((nb, ns), F32)
    kern = functools.partial(_s5_kernel, nb=nb, lc=lc, batch_major=batch_major)
    return pl.pallas_call(
        kern,
        grid=(L // lc,),
        in_specs=[x_spec, const((1, 1, d)), const((1, nb, d)), const((1, nb, d)), const((1, nb, d)),
                  const((nb, ns)), const((nb, ns)), const(win.shape), const(bre.shape), const(bim.shape),
                  const(cre.shape), const(cim.shape), const((1, ns)), const((1, ns)), const((1, d)),
                  const(wglu.shape)],
        out_specs=(x_spec, const((nb, ns)), const((nb, ns))),
        out_shape=(x_shape, st_shape, st_shape),
        scratch_shapes=[pltpu.VMEM((rows, d), F32), pltpu.VMEM((rows, d), F32),
                        pltpu.VMEM((rows, ns), F32), pltpu.VMEM((rows, ns), F32),
                        pltpu.VMEM((rows, d), F32)],
        compiler_params=_cparams(("arbitrary",), 56),
        name="s5",
    )(x, g.reshape(1, 1, d), mod3(sh), mod3(sc), mod3(gm), s0r, s0i, win, bre, bim, cre, cim,
      ar.reshape(1, ns), ai.reshape(1, ns), dsk.reshape(1, d), wglu)


def _router_kernel(x_ref, g_ref, sh_ref, sc_ref, wrt_ref, br_ref, cin_ref, h_ref, ti_ref, tg_ref, rk_ref, cnt_ref):
    @pl.when((pl.program_id(0) == 0) & (pl.program_id(1) == 0))
    def _():
        cnt_ref[...] = cin_ref[...]

    x = x_ref[0]
    h = _norm_mod(x, g_ref[0], sc_ref[0], sh_ref[0])
    hb = h.astype(BF16)
    h_ref[...] = h.reshape(h_ref.shape)
    h_lo = (h - hb.astype(F32)).astype(BF16)
    lt = (_dot_nt(wrt_ref[0], hb) + _dot_nt(wrt_ref[0], h_lo) + _dot_nt(wrt_ref[1], hb)
          + br_ref[...])
    ne = lt.shape[0]
    idx = lax.broadcasted_iota(jnp.int32, lt.shape, 0)
    vals, ids = [], []
    for _ in range(TOP_K):
        m = jnp.max(lt, axis=0, keepdims=True)
        i = jnp.min(jnp.where(lt == m, idx, ne), axis=0, keepdims=True)
        vals.append(m)
        ids.append(i)
        lt = jnp.where(idx == i, -jnp.inf, lt)
    es = [jnp.exp(v - vals[0]) for v in vals]
    tot = es[0] + es[1] + es[2] + es[3]
    rows = lt.shape[1]
    ti_ref[...] = jnp.concatenate(ids, axis=0)
    tg_ref[0] = jnp.concatenate([e / tot for e in es], axis=0)

    upper = jnp.where(lax.broadcasted_iota(jnp.int32, (rows, rows), 0) < lax.broadcasted_iota(jnp.int32, (rows, rows), 1),
                      1.0, 0.0).astype(BF16)
    run = cnt_ref[...]
    ranks = []
    for k in range(TOP_K):
        oh = jnp.where(idx == ids[k], 1.0, 0.0)
        pre = _dot(oh.astype(BF16), upper)
        ranks.append(jnp.sum(oh * (pre + run), axis=0, keepdims=True))
        run = run + jnp.sum(oh, axis=1, keepdims=True)
    cnt_ref[...] = run
    rk_ref[...] = jnp.concatenate(ranks, axis=0).astype(jnp.int32)


def _router(x3, g, sh3, sc3, wrt, br, cnt_in, *, tl):
    nbx, L, d = x3.shape
    lm = sh3.shape[1]
    ne = wrt.shape[1]
    nt = L // tl
    t = nbx * L
    mod_spec = (pl.BlockSpec((1, 1, d), lambda b, c: (b, 0, 0)) if lm == 1
                else pl.BlockSpec((1, tl, d), lambda b, c: (b, c, 0)))
    tok = lambda rows: pl.BlockSpec((rows, tl), lambda b, c: (0, b * nt + c))
    h, ti, tg, rk, cnt = pl.pallas_call(
        _router_kernel,
        grid=(nbx, nt),
        in_specs=[pl.BlockSpec((1, tl, d), lambda b, c: (b, c, 0)),
                  pl.BlockSpec((1, 1, d), lambda b, c: (0, 0, 0)), mod_spec, mod_spec,
                  pl.BlockSpec((2, ne, d), lambda b, c: (0, 0, 0)), pl.BlockSpec((ne, 1), lambda b, c: (0, 0)),
                  pl.BlockSpec((ne, 1), lambda b, c: (0, 0))],
        out_specs=(pl.BlockSpec((tl, d // LANES, LANES), lambda b, c: (b * nt + c, 0, 0)), tok(TOP_K),
                   pl.BlockSpec((1, TOP_K, tl), lambda b, c: (b * nt + c, 0, 0)),
                   tok(TOP_K), pl.BlockSpec((ne, 1), lambda b, c: (0, 0))),
        out_shape=(jax.ShapeDtypeStruct((t, d // LANES, LANES), F32), jax.ShapeDtypeStruct((TOP_K, t), jnp.int32),
                   jax.ShapeDtypeStruct((t // tl, TOP_K, tl), F32), jax.ShapeDtypeStruct((TOP_K, t), jnp.int32),
                   jax.ShapeDtypeStruct((ne, 1), F32)),
        compiler_params=_cparams(("arbitrary", "arbitrary"), 32),
        name="router",
    )(x3, g.reshape(1, 1, d), sh3, sc3, wrt, br, cnt_in)
    return h, ti, tg.reshape(-1), rk, cnt


def _pos_kernel(cnt_ref, ti_ref, rk_ref, pos_ref, *, tm):
    cnt = cnt_ref[...]
    ne = cnt.shape[0]
    tiles = jnp.floor((cnt + (tm - 1)) * (1.0 / tm))
    low = jnp.where(lax.broadcasted_iota(jnp.int32, (ne, ne), 1) < lax.broadcasted_iota(jnp.int32, (ne, ne), 0),
                    1.0, 0.0).astype(BF16)
    start = _dot(low, jnp.broadcast_to(tiles, (ne, LANES)).astype(BF16))[:, 0:1] * tm
    ti = ti_ref[...]
    idx = lax.broadcasted_iota(jnp.int32, (ne, ti.shape[1]), 0)
    sel = [jnp.sum(jnp.where(idx == ti[k:k + 1, :], start, 0.0), axis=0, keepdims=True) for k in range(TOP_K)]
    pos_ref[0] = jnp.concatenate(sel, axis=0).astype(jnp.int32) + rk_ref[...]


def _pos(cnt, ti, rk, *, tm, tl):
    k, t = ti.shape
    ne = cnt.shape[0]
    tok = pl.BlockSpec((k, tl), lambda i: (0, i))
    return pl.pallas_call(
        functools.partial(_pos_kernel, tm=tm),
        grid=(t // tl,),
        in_specs=[pl.BlockSpec((ne, 1), lambda i: (0, 0)), tok, tok],
        out_specs=pl.BlockSpec((1, k, tl), lambda i: (i, 0, 0)),
        out_shape=jax.ShapeDtypeStruct((t // tl, k, tl), jnp.int32),
        compiler_params=_cparams(("arbitrary",), 32),
        name="moe_pos",
    )(cnt, ti, rk).reshape(-1)


def _plan_kernel(cnt_ref, te_ref, tv_ref, tf_ref, *, tm, ne, n_tiles):
    shift = tm.bit_length() - 1

    def body(i, carry):
        e, end = carry

        def adv(c):
            e2 = c[0] + 1
            return e2, c[1] + ((cnt_ref[e2] + (tm - 1)) >> shift)

        e2, end2 = lax.while_loop(lambda c: (c[1] <= i) & (c[0] < ne - 1), adv, (e, end))
        te_ref[i] = e2
        tv_ref[i] = (i < end2).astype(jnp.int32)
        tf_ref[i] = ((e2 != e) | (i == 0)).astype(jnp.int32)
        return e2, end2

    lax.fori_loop(0, n_tiles, body, (jnp.int32(0), (cnt_ref[0] + (tm - 1)) >> shift))


def _plan(cnt_i32, *, tm, n_tiles):
    ne = cnt_i32.shape[0]
    smem = pl.BlockSpec(memory_space=pltpu.SMEM)
    out = jax.ShapeDtypeStruct((n_tiles,), jnp.int32)
    return pl.pallas_call(
        functools.partial(_plan_kernel, tm=tm, ne=ne, n_tiles=n_tiles),
        in_specs=[smem], out_specs=(smem, smem, smem), out_shape=(out, out, out), name="moe_plan",
    )(cnt_i32)


def _dispatch_kernel(pos_ref, h_ref, xs_in, xs_out, sem, *, tl):
    del xs_in

    def body(j, c):
        for u in range(SUBLANES):
            r = j * SUBLANES + u
            for k in range(TOP_K):
                pltpu.make_async_copy(h_ref.at[r], xs_out.at[pos_ref[k * tl + r]], sem).start(priority=k % 2)
        return c

    lax.fori_loop(0, tl // SUBLANES, body, 0)
    for k in range(TOP_K):
        pltpu.make_async_copy(h_ref, xs_out.at[pl.ds(0, tl)], sem).wait()


def _dispatch(pos, h, xs, *, tl):
    t, nj, _ = h.shape
    return pl.pallas_call(
        functools.partial(_dispatch_kernel, tl=tl),
        grid=(t // tl,),
        in_specs=[pl.BlockSpec((TOP_K * tl,), lambda i: (i,), memory_space=pltpu.SMEM),
                  pl.BlockSpec((tl, nj, LANES), lambda i: (i, 0, 0)), pl.BlockSpec(memory_space=pl.ANY)],
        out_specs=pl.BlockSpec(memory_space=pl.ANY),
        out_shape=jax.ShapeDtypeStruct(xs.shape, xs.dtype),
        scratch_shapes=[pltpu.SemaphoreType.DMA],
        input_output_aliases={2: 0},
        compiler_params=_cparams(("arbitrary",), 32),
        name="moe_dispatch",
    )(pos, h, xs)


def _combine_kernel(pos_ref, posn_ref, tg_ref, x_ref, gf_ref, fg_ref, outs_hbm, xo_ref, buf, sem, *,
                    tl, nsteps, final):
    i = pl.program_id(0)
    slot = i % 2
    nslot = 1 - slot
    d = x_ref.shape[2]

    def start_row(pref, s, r):
        for k in range(TOP_K):
            pltpu.make_async_copy(outs_hbm.at[pref[k * tl + r]], buf.at[s, k, r], sem.at[s]).start(priority=k % 2)

    def wait_tile(s):
        for k in range(TOP_K):
            pltpu.make_async_copy(outs_hbm.at[pl.ds(0, tl)], buf.at[s, k], sem.at[s]).wait()

    @pl.when(i == 0)
    def _():
        def body(j, c):
            for u in range(SUBLANES):
                start_row(pos_ref, 0, j * SUBLANES + u)
            return c

        lax.fori_loop(0, tl // SUBLANES, body, 0)

    wait_tile(slot)

    def body(j, c):
        r0 = pl.multiple_of(j * SUBLANES, SUBLANES)
        rows = pl.ds(r0, SUBLANES)
        ys = []
        for u in range(SUBLANES):
            r = r0 + u
            start_row(posn_ref, nslot, r)
            y = tg_ref[r] * buf[slot, 0, r]
            for k in range(1, TOP_K):
                y = y + tg_ref[k * tl + r] * buf[slot, k, r]
            ys.append(y)
        y = jnp.stack(ys, axis=0).reshape(SUBLANES, d)
        gf = gf_ref[0] if gf_ref.shape[1] == 1 else gf_ref[0, rows, :]
        xo_ref[0, rows, :] = x_ref[0, rows, :] + gf * y
        return c

    lax.fori_loop(0, tl // SUBLANES, body, 0)
    if final:
        xn = xo_ref[0]
        xo_ref[0] = xn * lax.rsqrt(jnp.mean(xn * xn, axis=-1, keepdims=True) + EPS) * fg_ref[...]

    @pl.when(i == nsteps - 1)
    def _():
        wait_tile(nslot)


def _combine(pos, x3, gf3, tg, fgain, outs, *, tl, final):
    nbx, L, d = x3.shape
    nt = L // tl
    nsteps = nbx * nt
    nj = d // LANES
    lm = gf3.shape[1]
    mod_spec = (pl.BlockSpec((1, 1, d), lambda i: (i // nt, 0, 0)) if lm == 1
                else pl.BlockSpec((1, tl, d), lambda i: (i // nt, i % nt, 0)))
    return pl.pallas_call(
        functools.partial(_combine_kernel, tl=tl, nsteps=nsteps, final=final),
        grid=(nsteps,),
        in_specs=[pl.BlockSpec((TOP_K * tl,), lambda i: (i,), memory_space=pltpu.SMEM),
                  pl.BlockSpec((TOP_K * tl,), lambda i: (jnp.minimum(i + 1, nsteps - 1),), memory_space=pltpu.SMEM),
                  pl.BlockSpec((TOP_K * tl,), lambda i: (i,), memory_space=pltpu.SMEM),
                  pl.BlockSpec((1, tl, d), lambda i: (i // nt, i % nt, 0)), mod_spec,
                  pl.BlockSpec((1, d), lambda i: (0, 0)), pl.BlockSpec(memory_space=pl.ANY)],
        out_specs=pl.BlockSpec((1, tl, d), lambda i: (i // nt, i % nt, 0)),
        out_shape=jax.ShapeDtypeStruct((nbx, L, d), F32),
        scratch_shapes=[pltpu.VMEM((2, TOP_K, tl, nj, LANES), F32), pltpu.SemaphoreType.DMA((2,))],
        compiler_params=_cparams(("arbitrary",), 48),
        name="moe_combine",
    )(pos, pos, tg, x3, gf3, fgain.reshape(1, d), outs)


def _gmm_kernel(te_ref, tv_ref, tf_ref, x_ref, wgu_ref, bgu_ref, wd_ref, bd_ref, o_ref,
                wgu_s, wd_s, act_s):
    i = pl.program_id(0)
    de = wd_s.shape[0]
    nch = 4
    cw = de // nch

    @pl.when(tf_ref[i] == 1)
    def _():
        for n in range(2 * nch):
            wgu_s[:, n * cw:(n + 1) * cw] = wgu_ref[0, :, n * cw:(n + 1) * cw].astype(BF16)
        for n in range(nch):
            wd_s[n * cw:(n + 1) * cw, :] = wd_ref[0, n * cw:(n + 1) * cw, :].astype(BF16)

    @pl.when(tv_ref[i] == 1)
    def _():
        x = x_ref[...].reshape(act_s.shape[0], wgu_s.shape[0]).astype(BF16)
        for n in range(nch):
            glu = _dot(x, wgu_s[:, n * cw:(n + 1) * cw]) + bgu_ref[0, :, n * cw:(n + 1) * cw]
            lin = _dot(x, wgu_s[:, de + n * cw:de + (n + 1) * cw]) + bgu_ref[0, :, de + n * cw:de + (n + 1) * cw]
            glu = jnp.minimum(glu, SWIGLU_LIMIT)
            lin = jnp.clip(lin, -SWIGLU_LIMIT, SWIGLU_LIMIT)
            act = glu * jax.nn.sigmoid(SWIGLU_ALPHA * glu) * (lin + 1.0)
            act_s[:, n * cw:(n + 1) * cw] = act.astype(BF16)
        o_ref[...] = (_dot(act_s[...], wd_s[...]) + bd_ref[0]).reshape(o_ref.shape)

    @pl.when(tv_ref[i] == 0)
    def _():
        o_ref[...] = jnp.zeros_like(o_ref)


def _gmm(xs, tile_expert, tile_valid, tile_first, w_gu, b_gu, w_down, b_down, *, tm, layer):
    n_slots, nj, _ = xs.shape
    depth, ne, d, de2 = w_gu.shape
    de = de2 // 2
    n_tiles = n_slots // tm
    b_gu = b_gu.reshape(depth * ne, 1, de2)
    b_down = b_down.reshape(depth * ne, 1, d)
    rows = pl.BlockSpec((tm, nj, LANES), lambda i, te, tv, tf: (i, 0, 0))
    grid_spec = pltpu.PrefetchScalarGridSpec(
        num_scalar_prefetch=3,
        grid=(n_tiles,),
        in_specs=[rows,
                  pl.BlockSpec((None, 1, d, de2), lambda i, te, tv, tf: (layer, te[i], 0, 0)),
                  pl.BlockSpec((1, 1, de2), lambda i, te, tv, tf: (layer * ne + te[i], 0, 0)),
                  pl.BlockSpec((None, 1, de, d), lambda i, te, tv, tf: (layer, te[i], 0, 0)),
                  pl.BlockSpec((1, 1, d), lambda i, te, tv, tf: (layer * ne + te[i], 0, 0))],
        out_specs=rows,
        scratch_shapes=[pltpu.VMEM((d, de2), BF16), pltpu.VMEM((de, d), BF16), pltpu.VMEM((tm, de), BF16)],
    )
    return pl.pallas_call(
        _gmm_kernel,
        grid_spec=grid_spec,
        out_shape=jax.ShapeDtypeStruct((n_slots, nj, LANES), F32),
        compiler_params=_cparams(("arbitrary",), 58),
        name="gmm",
    )(tile_expert, tile_valid, tile_first, xs, w_gu, b_gu, w_down, b_down)


def _moe_layer(xp, xs_, g, mods_p, mods_s, wr, br_, w_gu, b_gu, w_down, b_down, slots, fgain, *, layer, final):
    bp, seq, d = xp.shape
    bs = xs_.shape[0]
    ne = wr.shape[1]
    tm = GMM_ROWS
    n_tiles = -(-(TOP_K * (bp * seq + bs)) // tm) + ne
    sh_p, sc_p, gf_p = mods_p
    sh_s, sc_s, gf_s = mods_s
    wr_t = wr.T
    wr_hi = wr_t.astype(BF16)
    wrt = jnp.stack([wr_hi, (wr_t - wr_hi.astype(F32)).astype(BF16)])
    br = br_.reshape(ne, 1)
    cnt0 = jnp.zeros((ne, 1), F32)
    hp, tip, tgp, rkp, cnt1 = _router(xp, g, sh_p[:, None], sc_p[:, None], wrt, br, cnt0, tl=ROUTER_ROWS)
    hs, tis, tgs, rks, cnt = _router(xs_[None], g, sh_s[None], sc_s[None], wrt, br, cnt1, tl=bs)
    pos_p = _pos(cnt, tip, rkp, tm=tm, tl=ROUTER_ROWS)
    pos_s = _pos(cnt, tis, rks, tm=tm, tl=bs)
    te, tv, tf = _plan(cnt.reshape(ne).astype(jnp.int32), tm=tm, n_tiles=n_tiles)
    if slots is None:
        slots = jnp.zeros((n_tiles * tm, d // LANES, LANES), F32)
    slots = _dispatch(pos_p, hp, slots, tl=ROUTER_ROWS)
    slots = _dispatch(pos_s, hs, slots, tl=bs)
    outs = _gmm(slots, te, tv, tf, w_gu, b_gu, w_down, b_down, tm=tm, layer=layer)
    xp = _combine(pos_p, xp, gf_p[:, None], tgp, fgain, outs, tl=ROUTER_ROWS, final=final)
    xs_ = _combine(pos_s, xs_[None], gf_s[None], tgs, fgain, outs, tl=bs, final=final)[0]
    return xp, xs_, slots


def _mlstm_kernel(x_ref, g_ref, sh_ref, sc_ref, gm_ref, wup_ref, cw_ref, cb_ref, wq_ref, wk_ref, wv_ref,
                  wg_ref, bg_ref, lnw_ref, skip_ref, wdn_ref,
                  xo_ref, cst_ref, nst_ref, mst_ref, conv_ref,
                  xm_s, up_s, xc_s, q_s, k_s, v_s, *, lc, heads, dh):
    c = pl.program_id(1)
    inner = heads * dh
    pad = SUBLANES

    @pl.when(c == 0)
    def _():
        cst_ref[...] = jnp.zeros_like(cst_ref)
        nst_ref[...] = jnp.zeros_like(nst_ref)
        mst_ref[...] = jnp.zeros_like(mst_ref)
        xm_s[0:pad, :] = jnp.zeros((pad, inner), F32)

    x = x_ref[0]
    h = _norm_mod(x, g_ref[0], sc_ref[0], sh_ref[0]).astype(BF16)
    up_s[...] = _dot(h, wup_ref[...])
    xm_s[pad:pad + lc, :] = up_s[:, :inner]
    conv = cb_ref[...]
    for k in range(ML_CONV):
        off = pad - (ML_CONV - 1) + k
        conv = conv + cw_ref[k:k + 1, :] * xm_s[off:off + lc, :]
    xc_s[...] = _silu(conv)
    xm_s[0:pad, :] = xm_s[lc:lc + pad, :]
    conv_ref[0] = xm_s[0:pad, :]

    gacc = jnp.broadcast_to(bg_ref[...], (lc, LANES))
    for hd in range(heads):
        hs = slice(hd * dh, (hd + 1) * dh)
        xch = xc_s[:, hs].astype(BF16)
        q = _dot(xch, wq_ref[hd])
        k = _dot(xch, wk_ref[hd]) * (dh ** -0.5)
        v = _dot(up_s[:, hs].astype(BF16), wv_ref[hd])
        qb = q.astype(BF16)
        vb = v.astype(BF16)
        q_s[:, hs] = qb
        k_s[:, hs] = k
        v_s[:, hs] = vb
        gacc = gacc + _dot(qb, wg_ref[0, hd]) + _dot(k.astype(BF16), wg_ref[1, hd]) + _dot(vb, wg_ref[2, hd])

    lf = _log_sigmoid(gacc)
    row = lax.broadcasted_iota(jnp.int32, (lc, lc), 0)
    col = lax.broadcasted_iota(jnp.int32, (lc, lc), 1)
    causal = row >= col
    tri = jnp.where(causal, 1.0, 0.0).astype(BF16)
    hi = lf.astype(BF16)
    r1 = lf - hi.astype(F32)
    mid = r1.astype(BF16)
    lo = (r1 - mid.astype(F32)).astype(BF16)
    bcum = _dot(tri, hi) + _dot(tri, mid) + _dot(tri, lo)
    g_t = gacc.T
    b_t = bcum.T

    acc = jnp.zeros((lc, xo_ref.shape[2]), F32)
    for hd in range(heads):
        hs = slice(hd * dh, (hd + 1) * dh)
        ig_c = gacc[:, hd:hd + 1]
        b_c = bcum[:, heads + hd:heads + hd + 1]
        ig_r = g_t[hd:hd + 1, :]
        b_r = b_t[heads + hd:heads + hd + 1, :]
        m_prev = mst_ref[0, hd][:, 0:1]
        dm = jnp.where(causal, (b_c - b_r) + ig_r, -jnp.inf)
        inter = b_c + m_prev
        m_t = jnp.maximum(inter, jnp.max(dm, axis=1, keepdims=True))
        w_intra = jnp.exp(dm - m_t)
        w_inter = jnp.exp(inter - m_t)
        qb = q_s[:, hs]
        kf = k_s[:, hs]
        vb = v_s[:, hs]
        cmat = cst_ref[0, hd]
        nvec = nst_ref[0, hd]
        qk = _dot_nt(qb, kf.astype(BF16)) * w_intra
        num = _dot(qk.astype(BF16), vb) + w_inter * _dot(qb, cmat.astype(BF16))
        qn = jnp.sum(qb.astype(F32) * nvec.astype(BF16).astype(F32), axis=1, keepdims=True)
        den = jnp.sum(qk, axis=1, keepdims=True) + w_inter * qn
        hh = num / jnp.maximum(jnp.abs(den), jnp.exp(-m_t))

        bl = b_c[lc - 1:lc, :]
        m_new = m_t[lc - 1:lc, :]
        w_s = jnp.exp(bl - b_c + ig_c - m_new)
        decay = jnp.exp(bl + m_prev - m_new)
        kw = w_s * kf
        cst_ref[0, hd] = decay * cmat + _dot(kw.T.astype(BF16), vb)
        nst_ref[0, hd] = decay * nvec + jnp.sum(kw, axis=0, keepdims=True)
        mst_ref[0, hd] = jnp.broadcast_to(m_new, (1, LANES))

        mu = jnp.mean(hh, axis=1, keepdims=True)
        hc = hh - mu
        var = jnp.mean(hc * hc, axis=1, keepdims=True)
        hn = hc * lax.rsqrt(var + EPS) * lnw_ref[:, hs]
        o = (hn + skip_ref[:, hs] * xc_s[:, hs]) * _silu(up_s[:, inner + hd * dh:inner + (hd + 1) * dh])
        acc = acc + _dot(o.astype(BF16), wdn_ref[hs, :])
    xo_ref[0] = x + gm_ref[0] * acc


def _mlstm_prompt(x, g, sh, sc, gm, wts, *, lc):
    wup, cw, cb, wq, wk, wv, wg, bg, lnw, skip, wdn = wts
    nb, L, d = x.shape
    heads, dh, _ = wq.shape
    inner = heads * dh
    const = lambda shape: pl.BlockSpec(shape, lambda b, c: (0,) * len(shape), pipeline_mode=pl.Buffered(1))
    per_b = lambda shape: pl.BlockSpec(shape, lambda b, c: (b,) + (0,) * (len(shape) - 1))
    mod3 = lambda t: t.reshape(nb, 1, d)
    kern = functools.partial(_mlstm_kernel, lc=lc, heads=heads, dh=dh)
    return pl.pallas_call(
        kern,
        grid=(nb, L // lc),
        in_specs=[pl.BlockSpec((1, lc, d), lambda b, c: (b, c, 0)), const((1, 1, d)),
                  per_b((1, 1, d)), per_b((1, 1, d)), per_b((1, 1, d)),
                  const(wup.shape), const(cw.shape), const((1, inner)), const(wq.shape), const(wk.shape),
                  const(wv.shape), const(wg.shape), const((1, LANES)), const((1, inner)), const((1, inner)),
                  const(wdn.shape)],
        out_specs=(pl.BlockSpec((1, lc, d), lambda b, c: (b, c, 0)),
                   per_b((1, heads, dh, dh)), per_b((1, heads, 1, dh)), per_b((1, heads, 1, LANES)),
                   per_b((1, SUBLANES, inner))),
        out_shape=(jax.ShapeDtypeStruct((nb, L, d), F32),
                   jax.ShapeDtypeStruct((nb, heads, dh, dh), F32),
                   jax.ShapeDtypeStruct((nb, heads, 1, dh), F32),
                   jax.ShapeDtypeStruct((nb, heads, 1, LANES), F32),
                   jax.ShapeDtypeStruct((nb, SUBLANES, inner), F32)),
        scratch_shapes=[pltpu.VMEM((lc + SUBLANES, inner), F32), pltpu.VMEM((lc, 2 * inner), F32),
                        pltpu.VMEM((lc, inner), F32), pltpu.VMEM((lc, inner), BF16),
                        pltpu.VMEM((lc, inner), F32), pltpu.VMEM((lc, inner), BF16)],
        compiler_params=_cparams(("arbitrary", "arbitrary"), 58),
        name="mlstm",
    )(x, g.reshape(1, 1, d), mod3(sh), mod3(sc), mod3(gm), wup, cw, cb.reshape(1, inner), wq, wk, wv, wg, bg,
      lnw.reshape(1, inner), skip.reshape(1, inner), wdn)


def _mls_proj_kernel(x_ref, g_ref, sh_ref, sc_ref, wxm_ref, wz_ref, buf_ref, cw_ref, cb_ref, wq_ref, wk_ref,
                     wv_ref, wg_ref, bg_ref,
                     q_ref, k_ref, v_ref, xc_ref, z_ref, gate_ref, nbuf_ref):
    hd = pl.program_id(0)

    @pl.when(hd == 0)
    def _():
        gate_ref[...] = jnp.broadcast_to(bg_ref[...], gate_ref.shape)

    h = _norm_mod(x_ref[...], g_ref[...], sc_ref[...], sh_ref[...]).astype(BF16)
    xm = _dot(h, wxm_ref[...])
    z_ref[...] = _dot(h, wz_ref[...])
    conv = cb_ref[...] + cw_ref[ML_CONV - 1:ML_CONV, :] * xm
    for k in range(ML_CONV - 1):
        conv = conv + cw_ref[k:k + 1, :] * buf_ref[k]
    for k in range(ML_CONV - 2):
        nbuf_ref[k] = buf_ref[k + 1]
    nbuf_ref[ML_CONV - 2] = xm
    xc = _silu(conv)
    xc_ref[...] = xc
    xcb = xc.astype(BF16)
    dh = xm.shape[1]
    q = _dot(xcb, wq_ref[0])
    k = _dot(xcb, wk_ref[0]) * (dh ** -0.5)
    v = _dot(xm.astype(BF16), wv_ref[0])
    q_ref[...] = q
    k_ref[...] = k
    v_ref[...] = v
    gate_ref[...] += (_dot(q.astype(BF16), wg_ref[0, 0]) + _dot(k.astype(BF16), wg_ref[1, 0])
                      + _dot(v.astype(BF16), wg_ref[2, 0]))


def _mls_proj(x, g, sh, sc, buf, wts):
    wup, cw, cb, wq, wk, wv, wg, bg = wts
    nb, d = x.shape
    heads, dh, _ = wq.shape
    inner = heads * dh
    full = lambda shape: pl.BlockSpec(shape, lambda h: (0,) * len(shape))
    colblk = pl.BlockSpec((nb, dh), lambda h: (0, h))
    act = jax.ShapeDtypeStruct((nb, inner), F32)
    return pl.pallas_call(
        _mls_proj_kernel,
        grid=(heads,),
        in_specs=[full((nb, d)), full((1, d)), full((nb, d)), full((nb, d)),
                  pl.BlockSpec((d, dh), lambda h: (0, h)), pl.BlockSpec((d, dh), lambda h: (0, heads + h)),
                  pl.BlockSpec((ML_CONV - 1, nb, dh), lambda h: (0, 0, h)),
                  pl.BlockSpec((ML_CONV, dh), lambda h: (0, h)), pl.BlockSpec((1, dh), lambda h: (0, h)),
                  pl.BlockSpec((1, dh, dh), lambda h: (h, 0, 0)), pl.BlockSpec((1, dh, dh), lambda h: (h, 0, 0)),
                  pl.BlockSpec((1, dh, dh), lambda h: (h, 0, 0)),
                  pl.BlockSpec((3, 1, dh, LANES), lambda h: (0, h, 0, 0)), full((1, LANES))],
        out_specs=(colblk, colblk, colblk, colblk, colblk, full((nb, LANES)),
                   pl.BlockSpec((ML_CONV - 1, nb, dh), lambda h: (0, 0, h))),
        out_shape=(act, act, act, act, act, jax.ShapeDtypeStruct((nb, LANES), F32),
                   jax.ShapeDtypeStruct((ML_CONV - 1, nb, inner), F32)),
        compiler_params=_cparams(("arbitrary",), 32),
        name="mls_proj",
    )(x, g.reshape(1, d), sh, sc, wup, wup, buf, cw, cb.reshape(1, inner), wq, wk, wv, wg, bg)


def _mls_gate_scalars(ig, fg, m0):
    lf = _log_sigmoid(fg)
    inter = lf + m0
    m_t = jnp.maximum(inter, ig)
    return m_t, jnp.exp(ig - m_t), jnp.exp(inter - m_t)


def _mls_state_kernel(q_ref, k_ref, v_ref, ig_ref, fg_ref, m0_ref, c_ref, qc_ref, cn_ref, *, heads, dh):
    _, w_in, w_dec = _mls_gate_scalars(ig_ref[0], fg_ref[0], m0_ref[0])
    rowmask = lax.broadcasted_iota(jnp.int32, (LANES, dh), 0) == 0
    for hd in range(heads):
        hs = slice(hd * dh, (hd + 1) * dh)
        wi = w_in[:, hd:hd + 1]
        wd = w_dec[:, hd:hd + 1]
        cmat = c_ref[0, hd]
        q8 = jnp.broadcast_to(q_ref[0, :, hs], (SUBLANES, dh)).astype(BF16)
        qc_ref[0, :, hs] = _dot(q8, cmat.astype(BF16))[0:1, :]
        kw = jnp.where(rowmask, wi * k_ref[0, :, hs], 0.0)
        vv = jnp.where(rowmask, v_ref[0, :, hs], 0.0)
        cn_ref[0, hd] = wd * cmat + _dot(kw.T.astype(BF16), vv.astype(BF16))


def _mls_state(q, k, v, ig, fg, m0, cst):
    nb, inner = q.shape
    _, heads, dh, _ = cst.shape
    row = lambda w: pl.BlockSpec((1, 1, w), lambda b: (b, 0, 0))
    r3 = lambda t: t.reshape(nb, 1, t.shape[1])
    kern = functools.partial(_mls_state_kernel, heads=heads, dh=dh)
    qc, cn = pl.pallas_call(
        kern,
        grid=(nb,),
        in_specs=[row(inner), row(inner), row(inner), row(LANES), row(LANES), row(LANES),
                  pl.BlockSpec((1, heads, dh, dh), lambda b: (b, 0, 0, 0))],
        out_specs=(row(inner), pl.BlockSpec((1, heads, dh, dh), lambda b: (b, 0, 0, 0))),
        out_shape=(jax.ShapeDtypeStruct((nb, 1, inner), F32), jax.ShapeDtypeStruct(cst.shape, F32)),
        compiler_params=_cparams(("arbitrary",), 40),
        name="mls_state",
    )(r3(q), r3(k), r3(v), r3(ig), r3(fg), r3(m0), cst)
    return qc.reshape(nb, inner), cn


def _mls_post_kernel(x_ref, gm_ref, q_ref, k_ref, v_ref, xc_ref, z_ref, qc_ref, n_ref, ig_ref, fg_ref, m0_ref,
                     lnw_ref, skip_ref, wdn_ref, xo_ref, nn_ref, mn_ref, *, heads, dh):
    m_t, w_in, w_dec = _mls_gate_scalars(ig_ref[...], fg_ref[...], m0_ref[...])
    mn_ref[...] = m_t
    acc = jnp.zeros(xo_ref.shape, F32)
    for hd in range(heads):
        hs = slice(hd * dh, (hd + 1) * dh)
        wi = w_in[:, hd:hd + 1]
        wd = w_dec[:, hd:hd + 1]
        mt = m_t[:, hd:hd + 1]
        q = q_ref[:, hs]
        k = k_ref[:, hs]
        qr = q.astype(BF16).astype(F32)
        qk = jnp.sum(qr * k.astype(BF16).astype(F32), axis=1, keepdims=True) * wi
        nvec = n_ref[:, hs]
        num = qk.astype(BF16).astype(F32) * v_ref[:, hs].astype(BF16).astype(F32) + wd * qc_ref[:, hs]
        den = qk + wd * jnp.sum(qr * nvec.astype(BF16).astype(F32), axis=1, keepdims=True)
        hh = num / jnp.maximum(jnp.abs(den), jnp.exp(-mt))
        nn_ref[:, hs] = wd * nvec + wi * k
        mu = jnp.mean(hh, axis=1, keepdims=True)
        hc = hh - mu
        var = jnp.mean(hc * hc, axis=1, keepdims=True)
        hn = hc * lax.rsqrt(var + EPS) * lnw_ref[:, hs]
        o = (hn + skip_ref[:, hs] * xc_ref[:, hs]) * _silu(z_ref[:, hs])
        acc = acc + _dot(o.astype(BF16), wdn_ref[hs, :])
    xo_ref[...] = x_ref[...] + gm_ref[...] * acc


def _mls_post(x, gm, q, k, v, xc, z, qc, n0, ig, fg, m0, lnw, skip, wdn, *, heads, dh):
    nb, d = x.shape
    inner = heads * dh
    kern = functools.partial(_mls_post_kernel, heads=heads, dh=dh)
    return pl.pallas_call(
        kern,
        out_shape=(jax.ShapeDtypeStruct((nb, d), F32), jax.ShapeDtypeStruct((nb, inner), F32),
                   jax.ShapeDtypeStruct((nb, LANES), F32)),
        compiler_params=pltpu.CompilerParams(vmem_limit_bytes=40 * MIB),
        name="mls_post",
    )(x, gm, q, k, v, xc, z, qc, n0, ig, fg, m0, lnw.reshape(1, inner), skip.reshape(1, inner), wdn)


def _pad_lanes(t):
    return jnp.pad(t, ((0, 0), (0, LANES - t.shape[1])))


def kernel(x_prompt, x_sample, c_prompt, c_sample, state_s5_re, state_s5_im, state_mlstm_C, state_mlstm_n, state_mlstm_m, state_mlstm_conv, norm_mix_g, norm_ffn_g, final_norm_g, ada_w, ada_b, s5_w_in, s5_lam_re, s5_lam_im, s5_log_dt, s5_b_re, s5_b_im, s5_c_re, s5_c_im, s5_d, s5_w_glu, ml_w_up, ml_conv_w, ml_conv_b, ml_w_q, ml_w_k, ml_w_v, ml_w_gate, ml_b_gate, ml_ln_w, ml_skip, ml_w_down, moe_w_router, moe_b_router, moe_w_gu, moe_b_gu, moe_w_down, moe_b_down):
    bp, seq, d = x_prompt.shape
    bs = x_sample.shape[0]
    tp = bp * seq
    _, groups, pstate, gch = s5_b_re.shape
    ns = groups * pstate
    heads, dh = ml_w_q.shape[1], ml_w_q.shape[2]
    inner = heads * dh
    ne = moe_w_router.shape[2]

    mods = _ada(jnp.concatenate([c_prompt, c_sample], axis=0), ada_w, ada_b)

    def mod(i, j):
        m = mods[i, :, j * d:(j + 1) * d]
        return m[:bp], m[bp:]

    ar, ai, bbr, bbi = _s5_prep(s5_lam_re[0], s5_lam_im[0], s5_log_dt[0], s5_b_re[0], s5_b_im[0])
    nblk = d // LANES
    bre = _block_diag(bbr.reshape(groups, gch, pstate), nblk).astype(BF16)
    bim = _block_diag(bbi.reshape(groups, gch, pstate), nblk).astype(BF16)
    cre = _block_diag(s5_c_re[0].transpose(0, 2, 1), nblk).astype(BF16)
    cim = _block_diag(s5_c_im[0].transpose(0, 2, 1), nblk).astype(BF16)
    s5_wts = (s5_w_in[0].astype(BF16), bre, bim, cre, cim, ar.reshape(ns), ai.reshape(ns), s5_d[0],
              s5_w_glu[0].astype(BF16))
    (sh_p, sh_s), (sc_p, sc_s), (gm_p, gm_s) = mod(0, 0), mod(0, 1), mod(0, 2)
    zeros_st = jnp.zeros((bp, ns), F32)
    xp, p_re, p_im = _s5_layer(x_prompt, norm_mix_g[0], sh_p, sc_p, gm_p, zeros_st, zeros_st, s5_wts,
                               batch_major=True, lc=S5_CHUNK)
    xs_, s_re, s_im = _s5_layer(x_sample.reshape(bs, d), norm_mix_g[0], sh_s, sc_s, gm_s,
                                state_s5_re[0].reshape(bs, ns), state_s5_im[0].reshape(bs, ns), s5_wts,
                                batch_major=False, lc=1)

    def moe_layer(i, xp, xs_, slots, final):
        (sh_p, sh_s), (sc_p, sc_s), (gf_p, gf_s) = mod(i, 3), mod(i, 4), mod(i, 5)
        return _moe_layer(xp, xs_, norm_ffn_g[i], (sh_p, sc_p, gf_p), (sh_s, sc_s, gf_s), moe_w_router[i],
                          moe_b_router[i], moe_w_gu, moe_b_gu, moe_w_down, moe_b_down, slots,
                          final_norm_g, layer=i, final=final)

    xp, xs_, slots = moe_layer(0, xp, xs_, None, False)

    wg = _pad_lanes(ml_w_gate[0]).reshape(3, heads, dh, LANES).astype(BF16)
    bg = _pad_lanes(ml_b_gate[0][None])
    wup = ml_w_up[0].astype(BF16)
    wq, wk, wv = ml_w_q[0].astype(BF16), ml_w_k[0].astype(BF16), ml_w_v[0].astype(BF16)
    wdn = ml_w_down[0].astype(BF16)
    (sh_p, sh_s), (sc_p, sc_s), (gm_p, gm_s) = mod(1, 0), mod(1, 1), mod(1, 2)
    xp, p_c, p_n, p_m, p_tail = _mlstm_prompt(
        xp, norm_mix_g[1], sh_p, sc_p, gm_p,
        (wup, ml_conv_w[0], ml_conv_b[0], wq, wk, wv, wg, bg, ml_ln_w[0], ml_skip[0], wdn), lc=ML_CHUNK)
    p_n = p_n.reshape(bp, heads, dh)
    p_m = p_m[:, :, 0, 0]
    p_conv = p_tail[:, SUBLANES - (ML_CONV - 1):, :]

    buf = state_mlstm_conv[0].transpose(1, 0, 2)
    q, k, v, xc, z, gates, nbuf = _mls_proj(xs_, norm_mix_g[1], sh_s, sc_s, buf,
                                            (wup, ml_conv_w[0], ml_conv_b[0], wq, wk, wv, wg, bg))
    ig = _pad_lanes(gates[:, :heads])
    fg = _pad_lanes(gates[:, heads:2 * heads])
    m0 = _pad_lanes(state_mlstm_m[0])
    qc, s_c = _mls_state(q, k, v, ig, fg, m0, state_mlstm_C[0])
    xs_, s_n, s_m = _mls_post(xs_, gm_s, q, k, v, xc, z, qc, state_mlstm_n[0].reshape(bs, inner), ig, fg, m0,
                              ml_ln_w[0], ml_skip[0], wdn, heads=heads, dh=dh)
    s_n = s_n.reshape(bs, heads, dh)
    s_m = s_m[:, :heads]
    s_conv = nbuf.transpose(1, 0, 2)

    y_p, y_s, _ = moe_layer(1, xp, xs_, slots, True)
    y_s = y_s.reshape(bs, 1, d)
    return (y_p, y_s,
            p_re.reshape(1, bp, groups, pstate), p_im.reshape(1, bp, groups, pstate),
            p_c[None], p_n[None], p_m[None], p_conv[None],
            s_re.reshape(1, bs, groups, pstate), s_im.reshape(1, bs, groups, pstate),
            s_c[None], s_n[None], s_m[None], s_conv[None])
```

```python
import functools

import jax
import jax.numpy as jnp
from jax import lax
from jax.experimental import pallas as pl
from jax.experimental.pallas import tpu as pltpu

F32 = jnp.float32
BF16 = jnp.bfloat16
EPS = 1e-6
TOP_K = 4
SWIGLU_LIMIT = 7.0
SWIGLU_ALPHA = 1.702
ML_CONV = 4

LANES = 128
SUBLANES = 8
MIB = 1024 * 1024

S5_CHUNK = 64
ML_CHUNK = 256
ROUTER_ROWS = 512
GMM_ROWS = 512


def _cparams(semantics, vmem_mib):
    return pltpu.CompilerParams(dimension_semantics=semantics, vmem_limit_bytes=int(vmem_mib * MIB))


def _dot(a, b):
    return jnp.dot(a, b, preferred_element_type=F32)


def _dot_nt(a, b):
    return lax.dot_general(a, b, (((1,), (1,)), ((), ())), preferred_element_type=F32)


def _norm_mod(x, g, sc, sh):
    y = x * lax.rsqrt(jnp.mean(x * x, axis=-1, keepdims=True) + EPS)
    return (y * g) * (1.0 + sc) + sh


def _silu(x):
    return x * jax.nn.sigmoid(x)


def _log_sigmoid(x):
    return -(jnp.maximum(-x, 0.0) + jnp.log1p(jnp.exp(-jnp.abs(x))))


def _ada_kernel(c_ref, w_ref, b_ref, o_ref):
    c = c_ref[...]
    o_ref[0] = _dot(_silu(c).astype(BF16), w_ref[0].astype(BF16)) + b_ref[0]


def _ada(c_all, ada_w, ada_b):
    depth, d, n = ada_w.shape
    rows = c_all.shape[0]
    tn = n // 4
    return pl.pallas_call(
        _ada_kernel,
        grid=(depth, n // tn),
        in_specs=[pl.BlockSpec((rows, d), lambda i, j: (0, 0)),
                  pl.BlockSpec((1, d, tn), lambda i, j: (i, 0, j)),
                  pl.BlockSpec((1, 1, tn), lambda i, j: (i, 0, j))],
        out_specs=pl.BlockSpec((1, rows, tn), lambda i, j: (i, 0, j)),
        out_shape=jax.ShapeDtypeStruct((depth, rows, n), F32),
        compiler_params=_cparams(("arbitrary", "arbitrary"), 32),
        name="ada",
    )(c_all, ada_w, ada_b.reshape(depth, 1, n))


def _s5_prep_kernel(lr_ref, li_ref, ldt_ref, br_ref, bi_ref, ar_ref, ai_ref, bbr_ref, bbi_ref):
    dt = jnp.exp(ldt_ref[...])
    lr = lr_ref[...]
    li = li_ref[...]
    mag = jnp.exp(lr * dt)
    ar = mag * jnp.cos(li * dt)
    ai = mag * jnp.sin(li * dt)
    den = lr * lr + li * li
    nr = ar - 1.0
    wr = (nr * lr + ai * li) / den
    wi = (ai * lr - nr * li) / den
    br = br_ref[...]
    bi = bi_ref[...]
    ar_ref[...] = ar
    ai_ref[...] = ai
    bbr_ref[...] = wr * br - wi * bi
    bbi_ref[...] = wr * bi + wi * br


def _s5_prep(lam_re, lam_im, log_dt, b_re, b_im):
    g, p, c = b_re.shape
    rep = lambda t: jnp.repeat(t, c, axis=0)
    bt = lambda t: t.transpose(0, 2, 1).reshape(g * c, p)
    shp = jax.ShapeDtypeStruct((g * c, p), F32)
    ar, ai, bbr, bbi = pl.pallas_call(
        _s5_prep_kernel, out_shape=(shp, shp, shp, shp), name="s5_prep",
    )(rep(lam_re), rep(lam_im), rep(log_dt[:, None]), bt(b_re), bt(b_im))
    return ar[::c], ai[::c], bbr, bbi


def _block_diag(t, nblk):
    g, a, b = t.shape
    gl = g // nblk
    t4 = t.reshape(nblk, gl, a, b)
    eye = jnp.eye(gl, dtype=t.dtype)
    return jnp.einsum('jgab,gh->jgahb', t4, eye).reshape(nblk, gl * a, gl * b)


def _s5_kernel(x_ref, g_ref, sh_ref, sc_ref, gm_ref, s0r_ref, s0i_ref, win_ref, bre_ref, bim_ref,
               cre_ref, cim_ref, ar_ref, ai_ref, d_ref, wglu_ref,
               xo_ref, fr_ref, fi_ref,
               xt_ref, u_ref, sre_ref, sim_ref, y_ref, *, nb, lc, batch_major):
    c = pl.program_id(0)
    rows = nb * lc
    d = u_ref.shape[1]
    ns = sre_ref.shape[1]
    nblk = bre_ref.shape[0]
    kb = d // nblk
    sb = ns // nblk

    @pl.when(c == 0)
    def _():
        fr_ref[...] = s0r_ref[...]
        fi_ref[...] = s0i_ref[...]

    if batch_major:
        xt_ref[...] = jnp.swapaxes(x_ref[...], 0, 1).reshape(rows, d)
    else:
        xt_ref[...] = x_ref[...]
    x3 = xt_ref[...].reshape(lc, nb, d)
    h = _norm_mod(x3, g_ref[...], sc_ref[...], sh_ref[...]).reshape(rows, d).astype(BF16)
    u = _dot(h, win_ref[...])
    u_ref[...] = u
    ub = u.astype(BF16)
    cb = max(LANES, (4 * SUBLANES * LANES) // nb)

    def scan_block(cs):
        a_r = jnp.broadcast_to(ar_ref[:, cs], (nb, cb))
        a_i = jnp.broadcast_to(ai_ref[:, cs], (nb, cb))
        sr, si = fr_ref[:, cs], fi_ref[:, cs]
        for t in range(lc):
            rs = slice(t * nb, (t + 1) * nb)
            nr = a_r * sr - a_i * si + sre_ref[rs, cs]
            ni = a_r * si + a_i * sr + sim_ref[rs, cs]
            sre_ref[rs, cs] = nr
            sim_ref[rs, cs] = ni
            sr, si = nr, ni
        fr_ref[:, cs] = sr
        fi_ref[:, cs] = si

    for j in range(nblk):
        js = slice(j * sb, (j + 1) * sb)
        ks = slice(j * kb, (j + 1) * kb)
        uj = ub[:, ks]
        sre_ref[:, js] = _dot(uj, bre_ref[j])
        sim_ref[:, js] = _dot(uj, bim_ref[j])
        for c in range(sb // cb):
            scan_block(slice(j * sb + c * cb, j * sb + (c + 1) * cb))
        yj = _dot(sre_ref[:, js].astype(BF16), cre_ref[j]) - _dot(sim_ref[:, js].astype(BF16), cim_ref[j])
        y_ref[:, ks] = yj + d_ref[:, ks] * u_ref[:, ks]

    yg = jax.nn.gelu(y_ref[...]).astype(BF16)
    vg = _dot(yg, wglu_ref[...])
    out = vg[:, :d] * jax.nn.sigmoid(vg[:, d:])
    xn = xt_ref[...].reshape(lc, nb, d) + gm_ref[...] * out.reshape(lc, nb, d)
    if batch_major:
        xo_ref[...] = jnp.swapaxes(xn, 0, 1)
    else:
        xo_ref[...] = xn.reshape(rows, d)


def _s5_layer(x, g, sh, sc, gm, s0r, s0i, wts, *, batch_major, lc):
    win, bre, bim, cre, cim, ar, ai, dsk, wglu = wts
    nb = sh.shape[0]
    d = win.shape[0]
    ns = s0r.shape[1]
    if batch_major:
        L = x.shape[1]
        x_spec = pl.BlockSpec((nb, lc, d), lambda c: (0, c, 0))
        x_shape = jax.ShapeDtypeStruct((nb, L, d), F32)
    else:
        L = x.shape[0] // nb
        x_spec = pl.BlockSpec((nb * lc, d), lambda c: (c, 0))
        x_shape = jax.ShapeDtypeStruct((L * nb, d), F32)
    rows = nb * lc
    const = lambda shape: pl.BlockSpec(shape, lambda c: (0,) * len(shape))
    mod3 = lambda t: t.reshape(1, nb, d)
    st_shape = jax.ShapeDtypeStruct((nb, ns), F32)
    kern = functools.partial(_s5_kernel, nb=nb, lc=lc, batch_major=batch_major)
    return pl.pallas_call(
        kern,
        grid=(L // lc,),
        in_specs=[x_spec, const((1, 1, d)), const((1, nb, d)), const((1, nb, d)), const((1, nb, d)),
                  const((nb, ns)), const((nb, ns)), const(win.shape), const(bre.shape), const(bim.shape),
                  const(cre.shape), const(cim.shape), const((1, ns)), const((1, ns)), const((1, d)),
                  const(wglu.shape)],
        out_specs=(x_spec, const((nb, ns)), const((nb, ns))),
        out_shape=(x_shape, st_shape, st_shape),
        scratch_shapes=[pltpu.VMEM((rows, d), F32), pltpu.VMEM((rows, d), F32),
                        pltpu.VMEM((rows, ns), F32), pltpu.VMEM((rows, ns), F32),
                        pltpu.VMEM((rows, d), F32)],
        compiler_params=_cparams(("arbitrary",), 56),
        name="s5",
    )(x, g.reshape(1, 1, d), mod3(sh), mod3(sc), mod3(gm), s0r, s0i, win, bre, bim, cre, cim,
      ar.reshape(1, ns), ai.reshape(1, ns), dsk.reshape(1, d), wglu)


def _router_kernel(x_ref, g_ref, sh_ref, sc_ref, wrt_ref, br_ref, cin_ref, h_ref, ti_ref, tg_ref, rk_ref, cnt_ref):
    @pl.when((pl.program_id(0) == 0) & (pl.program_id(1) == 0))
    def _():
        cnt_ref[...] = cin_ref[...]

    x = x_ref[0]
    h = _norm_mod(x, g_ref[0], sc_ref[0], sh_ref[0])
    hb = h.astype(BF16)
    h_ref[...] = h.reshape(h_ref.shape)
    h_lo = (h - hb.astype(F32)).astype(BF16)
    ne = br_ref.shape[0]
    both = _dot_nt(wrt_ref[...].reshape(2 * ne, hb.shape[1]), hb)
    lt = both[:ne] + _dot_nt(wrt_ref[0], h_lo) + both[ne:] + br_ref[...]
    idx = lax.broadcasted_iota(jnp.int32, lt.shape, 0)
    vals, ids = [], []
    for _ in range(TOP_K):
        m = jnp.max(lt, axis=0, keepdims=True)
        i = jnp.min(jnp.where(lt == m, idx, ne), axis=0, keepdims=True)
        vals.append(m)
        ids.append(i)
        lt = jnp.where(idx == i, -jnp.inf, lt)
    es = [jnp.exp(v - vals[0]) for v in vals]
    tot = es[0] + es[1] + es[2] + es[3]
    rows = lt.shape[1]
    ti_ref[...] = jnp.concatenate(ids, axis=0)
    tg_ref[0] = jnp.concatenate([e / tot for e in es], axis=0)

    upper = jnp.where(lax.broadcasted_iota(jnp.int32, (rows, rows), 0) < lax.broadcasted_iota(jnp.int32, (rows, rows), 1),
                      1.0, 0.0).astype(BF16)
    ohs = [jnp.where(idx == ids[k], 1.0, 0.0) for k in range(TOP_K)]
    pre_all = _dot(jnp.concatenate(ohs, axis=0).astype(BF16), upper)
    run = cnt_ref[...]
    ranks = []
    for k in range(TOP_K):
        ranks.append(jnp.sum(ohs[k] * (pre_all[k * ne:(k + 1) * ne] + run), axis=0, keepdims=True))
        run = run + jnp.sum(ohs[k], axis=1, keepdims=True)
    cnt_ref[...] = run
    rk_ref[...] = jnp.concatenate(ranks, axis=0).astype(jnp.int32)


def _router(x3, g, sh3, sc3, wrt, br, cnt_in, *, tl):
    nbx, L, d = x3.shape
    lm = sh3.shape[1]
    ne = wrt.shape[1]
    nt = L // tl
    t = nbx * L
    mod_spec = (pl.BlockSpec((1, 1, d), lambda b, c: (b, 0, 0)) if lm == 1
                else pl.BlockSpec((1, tl, d), lambda b, c: (b, c, 0)))
    tok = lambda rows: pl.BlockSpec((rows, tl), lambda b, c: (0, b * nt + c))
    h, ti, tg, rk, cnt = pl.pallas_call(
        _router_kernel,
        grid=(nbx, nt),
        in_specs=[pl.BlockSpec((1, tl, d), lambda b, c: (b, c, 0)),
                  pl.BlockSpec((1, 1, d), lambda b, c: (0, 0, 0)), mod_spec, mod_spec,
                  pl.BlockSpec((2, ne, d), lambda b, c: (0, 0, 0)), pl.BlockSpec((ne, 1), lambda b, c: (0, 0)),
                  pl.BlockSpec((ne, 1), lambda b, c: (0, 0))],
        out_specs=(pl.BlockSpec((tl, d // LANES, LANES), lambda b, c: (b * nt + c, 0, 0)), tok(TOP_K),
                   pl.BlockSpec((1, TOP_K, tl), lambda b, c: (b * nt + c, 0, 0)),
                   tok(TOP_K), pl.BlockSpec((ne, 1), lambda b, c: (0, 0))),
        out_shape=(jax.ShapeDtypeStruct((t, d // LANES, LANES), F32), jax.ShapeDtypeStruct((TOP_K, t), jnp.int32),
                   jax.ShapeDtypeStruct((t // tl, TOP_K, tl), F32), jax.ShapeDtypeStruct((TOP_K, t), jnp.int32),
                   jax.ShapeDtypeStruct((ne, 1), F32)),
        compiler_params=_cparams(("arbitrary", "arbitrary"), 32),
        name="router",
    )(x3, g.reshape(1, 1, d), sh3, sc3, wrt, br, cnt_in)
    return h, ti, tg.reshape(-1), rk, cnt


def _pos_kernel(cnt_ref, ti_ref, rk_ref, pos_ref, *, tm):
    cnt = cnt_ref[...]
    ne = cnt.shape[0]
    tiles = jnp.floor((cnt + (tm - 1)) * (1.0 / tm))
    low = jnp.where(lax.broadcasted_iota(jnp.int32, (ne, ne), 1) < lax.broadcasted_iota(jnp.int32, (ne, ne), 0),
                    1.0, 0.0).astype(BF16)
    start = _dot(low, jnp.broadcast_to(tiles, (ne, LANES)).astype(BF16))[:, 0:1] * tm
    ti = ti_ref[...]
    idx = lax.broadcasted_iota(jnp.int32, (ne, ti.shape[1]), 0)
    sel = [jnp.sum(jnp.where(idx == ti[k:k + 1, :], start, 0.0), axis=0, keepdims=True) for k in range(TOP_K)]
    pos_ref[0] = jnp.concatenate(sel, axis=0).astype(jnp.int32) + rk_ref[...]


def _pos(cnt, ti, rk, *, tm, tl):
    k, t = ti.shape
    ne = cnt.shape[0]
    tok = pl.BlockSpec((k, tl), lambda i: (0, i))
    return pl.pallas_call(
        functools.partial(_pos_kernel, tm=tm),
        grid=(t // tl,),
        in_specs=[pl.BlockSpec((ne, 1), lambda i: (0, 0)), tok, tok],
        out_specs=pl.BlockSpec((1, k, tl), lambda i: (i, 0, 0)),
        out_shape=jax.ShapeDtypeStruct((t // tl, k, tl), jnp.int32),
        compiler_params=_cparams(("arbitrary",), 32),
        name="moe_pos",
    )(cnt, ti, rk).reshape(-1)


def _plan_kernel(cnt_ref, te_ref, tv_ref, tf_ref, *, tm, ne, n_tiles):
    shift = tm.bit_length() - 1

    def body(i, carry):
        e, end = carry

        def adv(c):
            e2 = c[0] + 1
            return e2, c[1] + ((cnt_ref[e2] + (tm - 1)) >> shift)

        e2, end2 = lax.while_loop(lambda c: (c[1] <= i) & (c[0] < ne - 1), adv, (e, end))
        first = end2 - ((cnt_ref[e2] + (tm - 1)) >> shift)
        te_ref[i] = e2
        tv_ref[i] = jnp.where(i < end2, jnp.minimum(cnt_ref[e2] - (i - first) * tm, tm), 0)
        tf_ref[i] = ((e2 != e) | (i == 0)).astype(jnp.int32)
        return e2, end2

    lax.fori_loop(0, n_tiles, body, (jnp.int32(0), (cnt_ref[0] + (tm - 1)) >> shift))


def _plan(cnt_i32, *, tm, n_tiles):
    ne = cnt_i32.shape[0]
    smem = pl.BlockSpec(memory_space=pltpu.SMEM)
    out = jax.ShapeDtypeStruct((n_tiles,), jnp.int32)
    return pl.pallas_call(
        functools.partial(_plan_kernel, tm=tm, ne=ne, n_tiles=n_tiles),
        in_specs=[smem], out_specs=(smem, smem, smem), out_shape=(out, out, out), name="moe_plan",
    )(cnt_i32)


def _dispatch_kernel(pos_ref, h_ref, xs_in, xs_out, sem, *, tl):
    del xs_in

    def body(j, c):
        for u in range(SUBLANES):
            r = j * SUBLANES + u
            for k in range(TOP_K):
                pltpu.make_async_copy(h_ref.at[r], xs_out.at[pos_ref[k * tl + r]], sem).start(priority=k % 2)
        return c

    lax.fori_loop(0, tl // SUBLANES, body, 0)
    for k in range(TOP_K):
        pltpu.make_async_copy(h_ref, xs_out.at[pl.ds(0, tl)], sem).wait()


def _dispatch(pos, h, xs, *, tl):
    t, nj, _ = h.shape
    return pl.pallas_call(
        functools.partial(_dispatch_kernel, tl=tl),
        grid=(t // tl,),
        in_specs=[pl.BlockSpec((TOP_K * tl,), lambda i: (i,), memory_space=pltpu.SMEM),
                  pl.BlockSpec((tl, nj, LANES), lambda i: (i, 0, 0)), pl.BlockSpec(memory_space=pl.ANY)],
        out_specs=pl.BlockSpec(memory_space=pl.ANY),
        out_shape=jax.ShapeDtypeStruct(xs.shape, xs.dtype),
        scratch_shapes=[pltpu.SemaphoreType.DMA],
        input_output_aliases={2: 0},
        compiler_params=_cparams(("arbitrary",), 32),
        name="moe_dispatch",
    )(pos, h, xs)


def _combine_kernel(pos_ref, posn_ref, tg_ref, x_ref, gf_ref, fg_ref, outs_hbm, xo_ref, buf, sem, *,
                    tl, nsteps, final):
    i = pl.program_id(0)
    slot = i % 2
    nslot = 1 - slot
    d = x_ref.shape[2]

    def start_row(pref, s, r):
        for k in range(TOP_K):
            pltpu.make_async_copy(outs_hbm.at[pref[k * tl + r]], buf.at[s, k, r], sem.at[s]).start(priority=k % 2)

    def wait_tile(s):
        for k in range(TOP_K):
            pltpu.make_async_copy(outs_hbm.at[pl.ds(0, tl)], buf.at[s, k], sem.at[s]).wait()

    @pl.when(i == 0)
    def _():
        def body(j, c):
            for u in range(SUBLANES):
                start_row(pos_ref, 0, j * SUBLANES + u)
            return c

        lax.fori_loop(0, tl // SUBLANES, body, 0)

    wait_tile(slot)

    def body(j, c):
        r0 = pl.multiple_of(j * SUBLANES, SUBLANES)
        rows = pl.ds(r0, SUBLANES)
        ys = []
        for u in range(SUBLANES):
            r = r0 + u
            start_row(posn_ref, nslot, r)
            y = tg_ref[r] * buf[slot, 0, r]
            for k in range(1, TOP_K):
                y = y + tg_ref[k * tl + r] * buf[slot, k, r]
            ys.append(y)
        y = jnp.stack(ys, axis=0).reshape(SUBLANES, d)
        gf = gf_ref[0] if gf_ref.shape[1] == 1 else gf_ref[0, rows, :]
        xo_ref[0, rows, :] = x_ref[0, rows, :] + gf * y
        return c

    lax.fori_loop(0, tl // SUBLANES, body, 0)
    if final:
        xn = xo_ref[0]
        xo_ref[0] = xn * lax.rsqrt(jnp.mean(xn * xn, axis=-1, keepdims=True) + EPS) * fg_ref[...]

    @pl.when(i == nsteps - 1)
    def _():
        wait_tile(nslot)


def _combine(pos, x3, gf3, tg, fgain, outs, *, tl, final):
    nbx, L, d = x3.shape
    nt = L // tl
    nsteps = nbx * nt
    nj = d // LANES
    lm = gf3.shape[1]
    mod_spec = (pl.BlockSpec((1, 1, d), lambda i: (i // nt, 0, 0)) if lm == 1
                else pl.BlockSpec((1, tl, d), lambda i: (i // nt, i % nt, 0)))
    return pl.pallas_call(
        functools.partial(_combine_kernel, tl=tl, nsteps=nsteps, final=final),
        grid=(nsteps,),
        in_specs=[pl.BlockSpec((TOP_K * tl,), lambda i: (i,), memory_space=pltpu.SMEM),
                  pl.BlockSpec((TOP_K * tl,), lambda i: (jnp.minimum(i + 1, nsteps - 1),), memory_space=pltpu.SMEM),
                  pl.BlockSpec((TOP_K * tl,), lambda i: (i,), memory_space=pltpu.SMEM),
                  pl.BlockSpec((1, tl, d), lambda i: (i // nt, i % nt, 0)), mod_spec,
                  pl.BlockSpec((1, d), lambda i: (0, 0)), pl.BlockSpec(memory_space=pl.ANY)],
        out_specs=pl.BlockSpec((1, tl, d), lambda i: (i // nt, i % nt, 0)),
        out_shape=jax.ShapeDtypeStruct((nbx, L, d), F32),
        scratch_shapes=[pltpu.VMEM((2, TOP_K, tl, nj, LANES), F32), pltpu.SemaphoreType.DMA((2,))],
        compiler_params=_cparams(("arbitrary",), 48),
        name="moe_combine",
    )(pos, pos, tg, x3, gf3, fgain.reshape(1, d), outs)


def _gmm_kernel(te_ref, tv_ref, tf_ref, x_ref, wgu_ref, bgu_ref, wd_ref, bd_ref, o_ref,
                wgu_s, wd_s, act_s):
    i = pl.program_id(0)
    de = wd_s.shape[0]
    nch = 4
    cw = de // nch

    @pl.when(tf_ref[i] == 1)
    def _():
        for n in range(2 * nch):
            wgu_s[:, n * cw:(n + 1) * cw] = wgu_ref[0, :, n * cw:(n + 1) * cw].astype(BF16)
        for n in range(nch):
            wd_s[n * cw:(n + 1) * cw, :] = wd_ref[0, n * cw:(n + 1) * cw, :].astype(BF16)

    tm = act_s.shape[0]
    half = tm // 2
    nrows = tv_ref[i]

    def mlp(m):
        x = x_ref[0:m].reshape(m, wgu_s.shape[0]).astype(BF16)
        for n in range(nch):
            glu = _dot(x, wgu_s[:, n * cw:(n + 1) * cw]) + bgu_ref[0, :, n * cw:(n + 1) * cw]
            lin = _dot(x, wgu_s[:, de + n * cw:de + (n + 1) * cw]) + bgu_ref[0, :, de + n * cw:de + (n + 1) * cw]
            glu = jnp.minimum(glu, SWIGLU_LIMIT)
            lin = jnp.clip(lin, -SWIGLU_LIMIT, SWIGLU_LIMIT)
            act = glu * jax.nn.sigmoid(SWIGLU_ALPHA * glu) * (lin + 1.0)
            act_s[0:m, n * cw:(n + 1) * cw] = act.astype(BF16)
        return (_dot(act_s[0:m, :], wd_s[...]) + bd_ref[0]).reshape((m,) + o_ref.shape[1:])

    @pl.when(nrows > half)
    def _():
        o_ref[...] = mlp(tm)

    @pl.when((nrows > 0) & (nrows <= half))
    def _():
        o_ref[0:half] = mlp(half)
        o_ref[half:tm] = jnp.zeros((tm - half,) + o_ref.shape[1:], F32)

    @pl.when(nrows == 0)
    def _():
        o_ref[...] = jnp.zeros_like(o_ref)


def _gmm(xs, tile_expert, tile_valid, tile_first, w_gu, b_gu, w_down, b_down, *, tm, layer):
    n_slots, nj, _ = xs.shape
    depth, ne, d, de2 = w_gu.shape
    de = de2 // 2
    n_tiles = n_slots // tm
    b_gu = b_gu.reshape(depth * ne, 1, de2)
    b_down = b_down.reshape(depth * ne, 1, d)
    rows = pl.BlockSpec((tm, nj, LANES), lambda i, te, tv, tf: (i, 0, 0))
    grid_spec = pltpu.PrefetchScalarGridSpec(
        num_scalar_prefetch=3,
        grid=(n_tiles,),
        in_specs=[rows,
                  pl.BlockSpec((None, 1, d, de2), lambda i, te, tv, tf: (layer, te[i], 0, 0)),
                  pl.BlockSpec((1, 1, de2), lambda i, te, tv, tf: (layer * ne + te[i], 0, 0)),
                  pl.BlockSpec((None, 1, de, d), lambda i, te, tv, tf: (layer, te[i], 0, 0)),
                  pl.BlockSpec((1, 1, d), lambda i, te, tv, tf: (layer * ne + te[i], 0, 0))],
        out_specs=rows,
        scratch_shapes=[pltpu.VMEM((d, de2), BF16), pltpu.VMEM((de, d), BF16), pltpu.VMEM((tm, de), BF16)],
    )
    return pl.pallas_call(
        _gmm_kernel,
        grid_spec=grid_spec,
        out_shape=jax.ShapeDtypeStruct((n_slots, nj, LANES), F32),
        compiler_params=_cparams(("arbitrary",), 58),
        name="gmm",
    )(tile_expert, tile_valid, tile_first, xs, w_gu, b_gu, w_down, b_down)


def _moe_layer(xp, xs_, g, mods_p, mods_s, wr, br_, w_gu, b_gu, w_down, b_down, slots, fgain, *, layer, final):
    bp, seq, d = xp.shape
    bs = xs_.shape[0]
    ne = wr.shape[1]
    tm = GMM_ROWS
    n_tiles = -(-(TOP_K * (bp * seq + bs)) // tm) + ne
    sh_p, sc_p, gf_p = mods_p
    sh_s, sc_s, gf_s = mods_s
    wr_t = wr.T
    wr_hi = wr_t.astype(BF16)
    wrt = jnp.stack([wr_hi, (wr_t - wr_hi.astype(F32)).astype(BF16)])
    br = br_.reshape(ne, 1)
    cnt0 = jnp.zeros((ne, 1), F32)
    hp, tip, tgp, rkp, cnt1 = _router(xp, g, sh_p[:, None], sc_p[:, None], wrt, br, cnt0, tl=ROUTER_ROWS)
    hs, tis, tgs, rks, cnt = _router(xs_[None], g, sh_s[None], sc_s[None], wrt, br, cnt1, tl=bs)
    pos_p = _pos(cnt, tip, rkp, tm=tm, tl=ROUTER_ROWS)
    pos_s = _pos(cnt, tis, rks, tm=tm, tl=bs)
    te, tv, tf = _plan(cnt.reshape(ne).astype(jnp.int32), tm=tm, n_tiles=n_tiles)
    if slots is None:
        slots = jnp.zeros((n_tiles * tm, d // LANES, LANES), F32)
    slots = _dispatch(pos_p, hp, slots, tl=ROUTER_ROWS)
    slots = _dispatch(pos_s, hs, slots, tl=bs)
    outs = _gmm(slots, te, tv, tf, w_gu, b_gu, w_down, b_down, tm=tm, layer=layer)
    xp = _combine(pos_p, xp, gf_p[:, None], tgp, fgain, outs, tl=ROUTER_ROWS, final=final)
    xs_ = _combine(pos_s, xs_[None], gf_s[None], tgs, fgain, outs, tl=bs, final=final)[0]
    return xp, xs_, slots


def _mlstm_kernel(x_ref, g_ref, sh_ref, sc_ref, gm_ref, wup_ref, cw_ref, cb_ref, wq_ref, wk_ref, wv_ref,
                  wg_ref, bg_ref, lnw_ref, skip_ref, wdn_ref,
                  xo_ref, cst_ref, nst_ref, mst_ref, conv_ref,
                  xm_s, up_s, xc_s, q_s, k_s, v_s, *, lc, heads, dh):
    c = pl.program_id(1)
    inner = heads * dh
    pad = SUBLANES

    @pl.when(c == 0)
    def _():
        cst_ref[...] = jnp.zeros_like(cst_ref)
        nst_ref[...] = jnp.zeros_like(nst_ref)
        mst_ref[...] = jnp.zeros_like(mst_ref)
        xm_s[0:pad, :] = jnp.zeros((pad, inner), F32)

    x = x_ref[0]
    h = _norm_mod(x, g_ref[0], sc_ref[0], sh_ref[0]).astype(BF16)
    up_s[...] = _dot(h, wup_ref[...])
    xm_s[pad:pad + lc, :] = up_s[:, :inner]
    conv = cb_ref[...]
    for k in range(ML_CONV):
        off = pad - (ML_CONV - 1) + k
        conv = conv + cw_ref[k:k + 1, :] * xm_s[off:off + lc, :]
    xc_s[...] = _silu(conv)
    xm_s[0:pad, :] = xm_s[lc:lc + pad, :]
    conv_ref[0] = xm_s[0:pad, :]

    gacc = jnp.broadcast_to(bg_ref[...], (lc, LANES))
    for hd in range(heads):
        hs = slice(hd * dh, (hd + 1) * dh)
        xch = xc_s[:, hs].astype(BF16)
        q = _dot(xch, wq_ref[hd])
        k = _dot(xch, wk_ref[hd]) * (dh ** -0.5)
        v = _dot(up_s[:, hs].astype(BF16), wv_ref[hd])
        qb = q.astype(BF16)
        vb = v.astype(BF16)
        q_s[:, hs] = qb
        k_s[:, hs] = k
        v_s[:, hs] = vb
        gacc = gacc + _dot(qb, wg_ref[0, hd]) + _dot(k.astype(BF16), wg_ref[1, hd]) + _dot(vb, wg_ref[2, hd])

    lf = _log_sigmoid(gacc)
    row = lax.broadcasted_iota(jnp.int32, (lc, lc), 0)
    col = lax.broadcasted_iota(jnp.int32, (lc, lc), 1)
    causal = row >= col
    tri = jnp.where(causal, 1.0, 0.0).astype(BF16)
    hi = lf.astype(BF16)
    r1 = lf - hi.astype(F32)
    mid = r1.astype(BF16)
    lo = (r1 - mid.astype(F32)).astype(BF16)
    bcum = _dot(tri, hi) + _dot(tri, mid) + _dot(tri, lo)
    g_t = gacc.T
    b_t = bcum.T

    acc = jnp.zeros((lc, xo_ref.shape[2]), F32)
    for hd in range(heads):
        hs = slice(hd * dh, (hd + 1) * dh)
        ig_c = gacc[:, hd:hd + 1]
        b_c = bcum[:, heads + hd:heads + hd + 1]
        ig_r = g_t[hd:hd + 1, :]
        b_r = b_t[heads + hd:heads + hd + 1, :]
        m_prev = mst_ref[0, hd][:, 0:1]
        dm = jnp.where(causal, (b_c - b_r) + ig_r, -jnp.inf)
        inter = b_c + m_prev
        m_t = jnp.maximum(inter, jnp.max(dm, axis=1, keepdims=True))
        w_intra = jnp.exp(dm - m_t)
        w_inter = jnp.exp(inter - m_t)
        qb = q_s[:, hs]
        kf = k_s[:, hs]
        vb = v_s[:, hs]
        cmat = cst_ref[0, hd]
        nvec = nst_ref[0, hd]
        qk = _dot_nt(qb, kf.astype(BF16)) * w_intra
        num = _dot(qk.astype(BF16), vb) + w_inter * _dot(qb, cmat.astype(BF16))
        qn = jnp.sum(qb.astype(F32) * nvec.astype(BF16).astype(F32), axis=1, keepdims=True)
        den = jnp.sum(qk, axis=1, keepdims=True) + w_inter * qn
        hh = num / jnp.maximum(jnp.abs(den), jnp.exp(-m_t))

        bl = b_c[lc - 1:lc, :]
        m_new = m_t[lc - 1:lc, :]
        w_s = jnp.exp(bl - b_c + ig_c - m_new)
        decay = jnp.exp(bl + m_prev - m_new)
        kw = w_s * kf
        cst_ref[0, hd] = decay * cmat + _dot(kw.T.astype(BF16), vb)
        nst_ref[0, hd] = decay * nvec + jnp.sum(kw, axis=0, keepdims=True)
        mst_ref[0, hd] = jnp.broadcast_to(m_new, (1, LANES))

        mu = jnp.mean(hh, axis=1, keepdims=True)
        hc = hh - mu
        var = jnp.mean(hc * hc, axis=1, keepdims=True)
        hn = hc * lax.rsqrt(var + EPS) * lnw_ref[:, hs]
        o = (hn + skip_ref[:, hs] * xc_s[:, hs]) * _silu(up_s[:, inner + hd * dh:inner + (hd + 1) * dh])
        acc = acc + _dot(o.astype(BF16), wdn_ref[hs, :])
    xo_ref[0] = x + gm_ref[0] * acc


def _mlstm_prompt(x, g, sh, sc, gm, wts, *, lc):
    wup, cw, cb, wq, wk, wv, wg, bg, lnw, skip, wdn = wts
    nb, L, d = x.shape
    heads, dh, _ = wq.shape
    inner = heads * dh
    const = lambda shape: pl.BlockSpec(shape, lambda b, c: (0,) * len(shape), pipeline_mode=pl.Buffered(1))
    per_b = lambda shape: pl.BlockSpec(shape, lambda b, c: (b,) + (0,) * (len(shape) - 1))
    mod3 = lambda t: t.reshape(nb, 1, d)
    kern = functools.partial(_mlstm_kernel, lc=lc, heads=heads, dh=dh)
    return pl.pallas_call(
        kern,
        grid=(nb, L // lc),
        in_specs=[pl.BlockSpec((1, lc, d), lambda b, c: (b, c, 0)), const((1, 1, d)),
                  per_b((1, 1, d)), per_b((1, 1, d)), per_b((1, 1, d)),
                  const(wup.shape), const(cw.shape), const((1, inner)), const(wq.shape), const(wk.shape),
                  const(wv.shape), const(wg.shape), const((1, LANES)), const((1, inner)), const((1, inner)),
                  const(wdn.shape)],
        out_specs=(pl.BlockSpec((1, lc, d), lambda b, c: (b, c, 0)),
                   per_b((1, heads, dh, dh)), per_b((1, heads, 1, dh)), per_b((1, heads, 1, LANES)),
                   per_b((1, SUBLANES, inner))),
        out_shape=(jax.ShapeDtypeStruct((nb, L, d), F32),
                   jax.ShapeDtypeStruct((nb, heads, dh, dh), F32),
                   jax.ShapeDtypeStruct((nb, heads, 1, dh), F32),
                   jax.ShapeDtypeStruct((nb, heads, 1, LANES), F32),
                   jax.ShapeDtypeStruct((nb, SUBLANES, inner), F32)),
        scratch_shapes=[pltpu.VMEM((lc + SUBLANES, inner), F32), pltpu.VMEM((lc, 2 * inner), F32),
                        pltpu.VMEM((lc, inner), F32), pltpu.VMEM((lc, inner), BF16),
                        pltpu.VMEM((lc, inner), F32), pltpu.VMEM((lc, inner), BF16)],
        compiler_params=_cparams(("arbitrary", "arbitrary"), 58),
        name="mlstm",
    )(x, g.reshape(1, 1, d), mod3(sh), mod3(sc), mod3(gm), wup, cw, cb.reshape(1, inner), wq, wk, wv, wg, bg,
      lnw.reshape(1, inner), skip.reshape(1, inner), wdn)


def _mls_proj_kernel(x_ref, g_ref, sh_ref, sc_ref, wxm_ref, wz_ref, buf_ref, cw_ref, cb_ref, wq_ref, wk_ref,
                     wv_ref, wg_ref, bg_ref,
                     q_ref, k_ref, v_ref, xc_ref, z_ref, gate_ref, nbuf_ref):
    hd = pl.program_id(0)

    @pl.when(hd == 0)
    def _():
        gate_ref[...] = jnp.broadcast_to(bg_ref[...], gate_ref.shape)

    h = _norm_mod(x_ref[...], g_ref[...], sc_ref[...], sh_ref[...]).astype(BF16)
    xm = _dot(h, wxm_ref[...])
    z_ref[...] = _dot(h, wz_ref[...])
    conv = cb_ref[...] + cw_ref[ML_CONV - 1:ML_CONV, :] * xm
    for k in range(ML_CONV - 1):
        conv = conv + cw_ref[k:k + 1, :] * buf_ref[k]
    for k in range(ML_CONV - 2):
        nbuf_ref[k] = buf_ref[k + 1]
    nbuf_ref[ML_CONV - 2] = xm
    xc = _silu(conv)
    xc_ref[...] = xc
    xcb = xc.astype(BF16)
    dh = xm.shape[1]
    q = _dot(xcb, wq_ref[0])
    k = _dot(xcb, wk_ref[0]) * (dh ** -0.5)
    v = _dot(xm.astype(BF16), wv_ref[0])
    q_ref[...] = q
    k_ref[...] = k
    v_ref[...] = v
    gate_ref[...] += (_dot(q.astype(BF16), wg_ref[0, 0]) + _dot(k.astype(BF16), wg_ref[1, 0])
                      + _dot(v.astype(BF16), wg_ref[2, 0]))


def _mls_proj(x, g, sh, sc, buf, wts):
    wup, cw, cb, wq, wk, wv, wg, bg = wts
    nb, d = x.shape
    heads, dh, _ = wq.shape
    inner = heads * dh
    full = lambda shape: pl.BlockSpec(shape, lambda h: (0,) * len(shape))
    colblk = pl.BlockSpec((nb, dh), lambda h: (0, h))
    act = jax.ShapeDtypeStruct((nb, inner), F32)
    return pl.pallas_call(
        _mls_proj_kernel,
        grid=(heads,),
        in_specs=[full((nb, d)), full((1, d)), full((nb, d)), full((nb, d)),
                  pl.BlockSpec((d, dh), lambda h: (0, h)), pl.BlockSpec((d, dh), lambda h: (0, heads + h)),
                  pl.BlockSpec((ML_CONV - 1, nb, dh), lambda h: (0, 0, h)),
                  pl.BlockSpec((ML_CONV, dh), lambda h: (0, h)), pl.BlockSpec((1, dh), lambda h: (0, h)),
                  pl.BlockSpec((1, dh, dh), lambda h: (h, 0, 0)), pl.BlockSpec((1, dh, dh), lambda h: (h, 0, 0)),
                  pl.BlockSpec((1, dh, dh), lambda h: (h, 0, 0)),
                  pl.BlockSpec((3, 1, dh, LANES), lambda h: (0, h, 0, 0)), full((1, LANES))],
        out_specs=(colblk, colblk, colblk, colblk, colblk, full((nb, LANES)),
                   pl.BlockSpec((ML_CONV - 1, nb, dh), lambda h: (0, 0, h))),
        out_shape=(act, act, act, act, act, jax.ShapeDtypeStruct((nb, LANES), F32),
                   jax.ShapeDtypeStruct((ML_CONV - 1, nb, inner), F32)),
        compiler_params=_cparams(("arbitrary",), 32),
        name="mls_proj",
    )(x, g.reshape(1, d), sh, sc, wup, wup, buf, cw, cb.reshape(1, inner), wq, wk, wv, wg, bg)


def _mls_gate_scalars(ig, fg, m0):
    lf = _log_sigmoid(fg)
    inter = lf + m0
    m_t = jnp.maximum(inter, ig)
    return m_t, jnp.exp(ig - m_t), jnp.exp(inter - m_t)


def _mls_state_kernel(q_ref, k_ref, v_ref, ig_ref, fg_ref, m0_ref, c_ref, qc_ref, cn_ref, *, heads, dh):
    _, w_in, w_dec = _mls_gate_scalars(ig_ref[0], fg_ref[0], m0_ref[0])
    rowmask = lax.broadcasted_iota(jnp.int32, (LANES, dh), 0) == 0
    for hd in range(heads):
        hs = slice(hd * dh, (hd + 1) * dh)
        wi = w_in[:, hd:hd + 1]
        wd = w_dec[:, hd:hd + 1]
        cmat = c_ref[0, hd]
        q8 = jnp.broadcast_to(q_ref[0, :, hs], (SUBLANES, dh)).astype(BF16)
        qc_ref[0, :, hs] = _dot(q8, cmat.astype(BF16))[0:1, :]
        kw = jnp.where(rowmask, wi * k_ref[0, :, hs], 0.0)
        vv = jnp.where(rowmask, v_ref[0, :, hs], 0.0)
        cn_ref[0, hd] = wd * cmat + _dot(kw.T.astype(BF16), vv.astype(BF16))


def _mls_state(q, k, v, ig, fg, m0, cst):
    nb, inner = q.shape
    _, heads, dh, _ = cst.shape
    row = lambda w: pl.BlockSpec((1, 1, w), lambda b: (b, 0, 0))
    r3 = lambda t: t.reshape(nb, 1, t.shape[1])
    kern = functools.partial(_mls_state_kernel, heads=heads, dh=dh)
    qc, cn = pl.pallas_call(
        kern,
        grid=(nb,),
        in_specs=[row(inner), row(inner), row(inner), row(LANES), row(LANES), row(LANES),
                  pl.BlockSpec((1, heads, dh, dh), lambda b: (b, 0, 0, 0))],
        out_specs=(row(inner), pl.BlockSpec((1, heads, dh, dh), lambda b: (b, 0, 0, 0))),
        out_shape=(jax.ShapeDtypeStruct((nb, 1, inner), F32), jax.ShapeDtypeStruct(cst.shape, F32)),
        compiler_params=_cparams(("arbitrary",), 40),
        name="mls_state",
    )(r3(q), r3(k), r3(v), r3(ig), r3(fg), r3(m0), cst)
    return qc.reshape(nb, inner), cn


def _mls_post_kernel(x_ref, gm_ref, q_ref, k_ref, v_ref, xc_ref, z_ref, qc_ref, n_ref, ig_ref, fg_ref, m0_ref,
                     lnw_ref, skip_ref, wdn_ref, xo_ref, nn_ref, mn_ref, *, heads, dh):
    m_t, w_in, w_dec = _mls_gate_scalars(ig_ref[...], fg_ref[...], m0_ref[...])
    mn_ref[...] = m_t
    acc = jnp.zeros(xo_ref.shape, F32)
    for hd in range(heads):
        hs = slice(hd * dh, (hd + 1) * dh)
        wi = w_in[:, hd:hd + 1]
        wd = w_dec[:, hd:hd + 1]
        mt = m_t[:, hd:hd + 1]
        q = q_ref[:, hs]
        k = k_ref[:, hs]
        qr = q.astype(BF16).astype(F32)
        qk = jnp.sum(qr * k.astype(BF16).astype(F32), axis=1, keepdims=True) * wi
        nvec = n_ref[:, hs]
        num = qk.astype(BF16).astype(F32) * v_ref[:, hs].astype(BF16).astype(F32) + wd * qc_ref[:, hs]
        den = qk + wd * jnp.sum(qr * nvec.astype(BF16).astype(F32), axis=1, keepdims=True)
        hh = num / jnp.maximum(jnp.abs(den), jnp.exp(-mt))
        nn_ref[:, hs] = wd * nvec + wi * k
        mu = jnp.mean(hh, axis=1, keepdims=True)
        hc = hh - mu
        var = jnp.mean(hc * hc, axis=1, keepdims=True)
        hn = hc * lax.rsqrt(var + EPS) * lnw_ref[:, hs]
        o = (hn + skip_ref[:, hs] * xc_ref[:, hs]) * _silu(z_ref[:, hs])
        acc = acc + _dot(o.astype(BF16), wdn_ref[hs, :])
    xo_ref[...] = x_ref[...] + gm_ref[...] * acc


def _mls_post(x, gm, q, k, v, xc, z, qc, n0, ig, fg, m0, lnw, skip, wdn, *, heads, dh):
    nb, d = x.shape
    inner = heads * dh
    kern = functools.partial(_mls_post_kernel, heads=heads, dh=dh)
    return pl.pallas_call(
        kern,
        out_shape=(jax.ShapeDtypeStruct((nb, d), F32), jax.ShapeDtypeStruct((nb, inner), F32),
                   jax.ShapeDtypeStruct((nb, LANES), F32)),
        compiler_params=pltpu.CompilerParams(vmem_limit_bytes=40 * MIB),
        name="mls_post",
    )(x, gm, q, k, v, xc, z, qc, n0, ig, fg, m0, lnw.reshape(1, inner), skip.reshape(1, inner), wdn)


def _pad_lanes(t):
    return jnp.pad(t, ((0, 0), (0, LANES - t.shape[1])))


def kernel(x_prompt, x_sample, c_prompt, c_sample, state_s5_re, state_s5_im, state_mlstm_C, state_mlstm_n, state_mlstm_m, state_mlstm_conv, norm_mix_g, norm_ffn_g, final_norm_g, ada_w, ada_b, s5_w_in, s5_lam_re, s5_lam_im, s5_log_dt, s5_b_re, s5_b_im, s5_c_re, s5_c_im, s5_d, s5_w_glu, ml_w_up, ml_conv_w, ml_conv_b, ml_w_q, ml_w_k, ml_w_v, ml_w_gate, ml_b_gate, ml_ln_w, ml_skip, ml_w_down, moe_w_router, moe_b_router, moe_w_gu, moe_b_gu, moe_w_down, moe_b_down):
    bp, seq, d = x_prompt.shape
    bs = x_sample.shape[0]
    tp = bp * seq
    _, groups, pstate, gch = s5_b_re.shape
    ns = groups * pstate
    heads, dh = ml_w_q.shape[1], ml_w_q.shape[2]
    inner = heads * dh
    ne = moe_w_router.shape[2]

    mods = _ada(jnp.concatenate([c_prompt, c_sample], axis=0), ada_w, ada_b)

    def mod(i, j):
        m = mods[i, :, j * d:(j + 1) * d]
        return m[:bp], m[bp:]

    ar, ai, bbr, bbi = _s5_prep(s5_lam_re[0], s5_lam_im[0], s5_log_dt[0], s5_b_re[0], s5_b_im[0])
    nblk = d // LANES
    bre = _block_diag(bbr.reshape(groups, gch, pstate), nblk).astype(BF16)
    bim = _block_diag(bbi.reshape(groups, gch, pstate), nblk).astype(BF16)
    cre = _block_diag(s5_c_re[0].transpose(0, 2, 1), nblk).astype(BF16)
    cim = _block_diag(s5_c_im[0].transpose(0, 2, 1), nblk).astype(BF16)
    s5_wts = (s5_w_in[0].astype(BF16), bre, bim, cre, cim, ar.reshape(ns), ai.reshape(ns), s5_d[0],
              s5_w_glu[0].astype(BF16))
    (sh_p, sh_s), (sc_p, sc_s), (gm_p, gm_s) = mod(0, 0), mod(0, 1), mod(0, 2)
    zeros_st = jnp.zeros((bp, ns), F32)
    xp, p_re, p_im = _s5_layer(x_prompt, norm_mix_g[0], sh_p, sc_p, gm_p, zeros_st, zeros_st, s5_wts,
                               batch_major=True, lc=S5_CHUNK)
    xs_, s_re, s_im = _s5_layer(x_sample.reshape(bs, d), norm_mix_g[0], sh_s, sc_s, gm_s,
                                state_s5_re[0].reshape(bs, ns), state_s5_im[0].reshape(bs, ns), s5_wts,
                                batch_major=False, lc=1)

    def moe_layer(i, xp, xs_, slots, final):
        (sh_p, sh_s), (sc_p, sc_s), (gf_p, gf_s) = mod(i, 3), mod(i, 4), mod(i, 5)
        return _moe_layer(xp, xs_, norm_ffn_g[i], (sh_p, sc_p, gf_p), (sh_s, sc_s, gf_s), moe_w_router[i],
                          moe_b_router[i], moe_w_gu, moe_b_gu, moe_w_down, moe_b_down, slots,
                          final_norm_g, layer=i, final=final)

    xp, xs_, slots = moe_layer(0, xp, xs_, None, False)

    wg = _pad_lanes(ml_w_gate[0]).reshape(3, heads, dh, LANES).astype(BF16)
    bg = _pad_lanes(ml_b_gate[0][None])
    wup = ml_w_up[0].astype(BF16)
    wq, wk, wv = ml_w_q[0].astype(BF16), ml_w_k[0].astype(BF16), ml_w_v[0].astype(BF16)
    wdn = ml_w_down[0].astype(BF16)
    (sh_p, sh_s), (sc_p, sc_s), (gm_p, gm_s) = mod(1, 0), mod(1, 1), mod(1, 2)
    xp, p_c, p_n, p_m, p_tail = _mlstm_prompt(
        xp, norm_mix_g[1], sh_p, sc_p, gm_p,
        (wup, ml_conv_w[0], ml_conv_b[0], wq, wk, wv, wg, bg, ml_ln_w[0], ml_skip[0], wdn), lc=ML_CHUNK)
    p_n = p_n.reshape(bp, heads, dh)
    p_m = p_m[:, :, 0, 0]
    p_conv = p_tail[:, SUBLANES - (ML_CONV - 1):, :]

    buf = state_mlstm_conv[0].transpose(1, 0, 2)
    q, k, v, xc, z, gates, nbuf = _mls_proj(xs_, norm_mix_g[1], sh_s, sc_s, buf,
                                            (wup, ml_conv_w[0], ml_conv_b[0], wq, wk, wv, wg, bg))
    ig = _pad_lanes(gates[:, :heads])
    fg = _pad_lanes(gates[:, heads:2 * heads])
    m0 = _pad_lanes(state_mlstm_m[0])
    qc, s_c = _mls_state(q, k, v, ig, fg, m0, state_mlstm_C[0])
    xs_, s_n, s_m = _mls_post(xs_, gm_s, q, k, v, xc, z, qc, state_mlstm_n[0].reshape(bs, inner), ig, fg, m0,
                              ml_ln_w[0], ml_skip[0], wdn, heads=heads, dh=dh)
    s_n = s_n.reshape(bs, heads, dh)
    s_m = s_m[:, :heads]
    s_conv = nbuf.transpose(1, 0, 2)

    y_p, y_s, _ = moe_layer(1, xp, xs_, slots, True)
    y_s = y_s.reshape(bs, 1, d)
    return (y_p, y_s,
            p_re.reshape(1, bp, groups, pstate), p_im.reshape(1, bp, groups, pstate),
            p_c[None], p_n[None], p_m[None], p_conv[None],
            s_re.reshape(1, bs, groups, pstate), s_im.reshape(1, bs, groups, pstate),
            s_c[None], s_n[None], s_m[None], s_conv[None])
```

```python
import functools

import jax
import jax.numpy as jnp
from jax import lax
from jax.experimental import pallas as pl
from jax.experimental.pallas import tpu as pltpu

F32 = jnp.float32
BF16 = jnp.bfloat16
EPS = 1e-6
TOP_K = 4
SWIGLU_LIMIT = 7.0
SWIGLU_ALPHA = 1.702
ML_CONV = 4

LANES = 128
SUBLANES = 8
MIB = 1024 * 1024

S5_CHUNK = 64
ML_CHUNK = 256
ROUTER_ROWS = 512
GMM_ROWS = 512
NORM_ROWS = 512


def _cparams(semantics, vmem_mib):
    return pltpu.CompilerParams(dimension_semantics=semantics, vmem_limit_bytes=int(vmem_mib * MIB))


def _dot(a, b):
    return jnp.dot(a, b, preferred_element_type=F32)


def _dot_nt(a, b):
    return lax.dot_general(a, b, (((1,), (1,)), ((), ())), preferred_element_type=F32)


def _norm_mod(x, g, sc, sh):
    y = x * lax.rsqrt(jnp.mean(x * x, axis=-1, keepdims=True) + EPS)
    return (y * g) * (1.0 + sc) + sh


def _silu(x):
    return x * jax.nn.sigmoid(x)


def _log_sigmoid(x):
    return -(jnp.maximum(-x, 0.0) + jnp.log1p(jnp.exp(-jnp.abs(x))))


def _ada_kernel(c_ref, w_ref, b_ref, o_ref):
    c = c_ref[...]
    o_ref[0] = _dot(_silu(c).astype(BF16), w_ref[0].astype(BF16)) + b_ref[0]


def _ada(c_all, ada_w, ada_b):
    depth, d, n = ada_w.shape
    rows = c_all.shape[0]
    tn = n // 4
    return pl.pallas_call(
        _ada_kernel,
        grid=(depth, n // tn),
        in_specs=[pl.BlockSpec((rows, d), lambda i, j: (0, 0)),
                  pl.BlockSpec((1, d, tn), lambda i, j: (i, 0, j)),
                  pl.BlockSpec((1, 1, tn), lambda i, j: (i, 0, j))],
        out_specs=pl.BlockSpec((1, rows, tn), lambda i, j: (i, 0, j)),
        out_shape=jax.ShapeDtypeStruct((depth, rows, n), F32),
        compiler_params=_cparams(("arbitrary", "arbitrary"), 32),
        name="ada",
    )(c_all, ada_w, ada_b.reshape(depth, 1, n))


def _s5_prep_kernel(lr_ref, li_ref, ldt_ref, br_ref, bi_ref, ar_ref, ai_ref, bbr_ref, bbi_ref):
    dt = jnp.exp(ldt_ref[...])
    lr = lr_ref[...]
    li = li_ref[...]
    mag = jnp.exp(lr * dt)
    ar = mag * jnp.cos(li * dt)
    ai = mag * jnp.sin(li * dt)
    den = lr * lr + li * li
    nr = ar - 1.0
    wr = (nr * lr + ai * li) / den
    wi = (ai * lr - nr * li) / den
    br = br_ref[...]
    bi = bi_ref[...]
    ar_ref[...] = ar
    ai_ref[...] = ai
    bbr_ref[...] = wr * br - wi * bi
    bbi_ref[...] = wr * bi + wi * br


def _s5_prep(lam_re, lam_im, log_dt, b_re, b_im):
    g, p, c = b_re.shape
    rep = lambda t: jnp.repeat(t, c, axis=0)
    bt = lambda t: t.transpose(0, 2, 1).reshape(g * c, p)
    shp = jax.ShapeDtypeStruct((g * c, p), F32)
    ar, ai, bbr, bbi = pl.pallas_call(
        _s5_prep_kernel, out_shape=(shp, shp, shp, shp), name="s5_prep",
    )(rep(lam_re), rep(lam_im), rep(log_dt[:, None]), bt(b_re), bt(b_im))
    return ar[::c], ai[::c], bbr, bbi


def _block_diag(t, nblk):
    g, a, b = t.shape
    gl = g // nblk
    t4 = t.reshape(nblk, gl, a, b)
    eye = jnp.eye(gl, dtype=t.dtype)
    return jnp.einsum('jgab,gh->jgahb', t4, eye).reshape(nblk, gl * a, gl * b)


def _s5_kernel(x_ref, g_ref, sh_ref, sc_ref, gm_ref, s0r_ref, s0i_ref, win_ref, bre_ref, bim_ref,
               cre_ref, cim_ref, ar_ref, ai_ref, d_ref, wglu_ref,
               xo_ref, fr_ref, fi_ref,
               xt_ref, u_ref, sre_ref, sim_ref, y_ref, *, nb, lc, batch_major):
    c = pl.program_id(0)
    rows = nb * lc
    d = u_ref.shape[1]
    ns = sre_ref.shape[1]
    nblk = bre_ref.shape[0]
    kb = d // nblk
    sb = ns // nblk

    @pl.when(c == 0)
    def _():
        fr_ref[...] = s0r_ref[...]
        fi_ref[...] = s0i_ref[...]

    if batch_major:
        xt_ref[...] = jnp.swapaxes(x_ref[...], 0, 1).reshape(rows, d)
    else:
        xt_ref[...] = x_ref[...]
    x3 = xt_ref[...].reshape(lc, nb, d)
    h = _norm_mod(x3, g_ref[...], sc_ref[...], sh_ref[...]).reshape(rows, d).astype(BF16)
    u = _dot(h, win_ref[...])
    u_ref[...] = u
    ub = u.astype(BF16)
    cb = max(LANES, (4 * SUBLANES * LANES) // nb)

    def scan_block(cs):
        a_r = jnp.broadcast_to(ar_ref[:, cs], (nb, cb))
        a_i = jnp.broadcast_to(ai_ref[:, cs], (nb, cb))
        sr, si = fr_ref[:, cs], fi_ref[:, cs]
        for t in range(lc):
            rs = slice(t * nb, (t + 1) * nb)
            nr = a_r * sr - a_i * si + sre_ref[rs, cs]
            ni = a_r * si + a_i * sr + sim_ref[rs, cs]
            sre_ref[rs, cs] = nr
            sim_ref[rs, cs] = ni
            sr, si = nr, ni
        fr_ref[:, cs] = sr
        fi_ref[:, cs] = si

    for j in range(nblk):
        js = slice(j * sb, (j + 1) * sb)
        ks = slice(j * kb, (j + 1) * kb)
        uj = ub[:, ks]
        sre_ref[:, js] = _dot(uj, bre_ref[j])
        sim_ref[:, js] = _dot(uj, bim_ref[j])
        for c in range(sb // cb):
            scan_block(slice(j * sb + c * cb, j * sb + (c + 1) * cb))
        yj = _dot(sre_ref[:, js].astype(BF16), cre_ref[j]) - _dot(sim_ref[:, js].astype(BF16), cim_ref[j])
        y_ref[:, ks] = yj + d_ref[:, ks] * u_ref[:, ks]

    yg = jax.nn.gelu(y_ref[...]).astype(BF16)
    vg = _dot(yg, wglu_ref[...])
    out = vg[:, :d] * jax.nn.sigmoid(vg[:, d:])
    xn = xt_ref[...].reshape(lc, nb, d) + gm_ref[...] * out.reshape(lc, nb, d)
    if batch_major:
        xo_ref[...] = jnp.swapaxes(xn, 0, 1)
    else:
        xo_ref[...] = xn.reshape(rows, d)


def _s5_layer(x, g, sh, sc, gm, s0r, s0i, wts, *, batch_major, lc):
    win, bre, bim, cre, cim, ar, ai, dsk, wglu = wts
    nb = sh.shape[0]
    d = win.shape[0]
    ns = s0r.shape[1]
    if batch_major:
        L = x.shape[1]
        x_spec = pl.BlockSpec((nb, lc, d), lambda c: (0, c, 0))
        x_shape = jax.ShapeDtypeStruct((nb, L, d), F32)
    else:
        L = x.shape[0] // nb
        x_spec = pl.BlockSpec((nb * lc, d), lambda c: (c, 0))
        x_shape = jax.ShapeDtypeStruct((L * nb, d), F32)
    rows = nb * lc
    const = lambda shape: pl.BlockSpec(shape, lambda c: (0,) * len(shape))
    mod3 = lambda t: t.reshape(1, nb, d)
    st_shape = jax.ShapeDtypeStruct((nb, ns), F32)
    kern = functools.partial(_s5_kernel, nb=nb, lc=lc, batch_major=batch_major)
    return pl.pallas_call(
        kern,
        grid=(L // lc,),
        in_specs=[x_spec, const((1, 1, d)), const((1, nb, d)), const((1, nb, d)), const((1, nb, d)),
                  const((nb, ns)), const((nb, ns)), const(win.shape), const(bre.shape), const(bim.shape),
                  const(cre.shape), const(cim.shape), const((1, ns)), const((1, ns)), const((1, d)),
                  const(wglu.shape)],
        out_specs=(x_spec, const((nb, ns)), const((nb, ns))),
        out_shape=(x_shape, st_shape, st_shape),
        scratch_shapes=[pltpu.VMEM((rows, d), F32), pltpu.VMEM((rows, d), F32),
                        pltpu.VMEM((rows, ns), F32), pltpu.VMEM((rows, ns), F32),
                        pltpu.VMEM((rows, d), F32)],
        compiler_params=_cparams(("arbitrary",), 56),
        name="s5",
    )(x, g.reshape(1, 1, d), mod3(sh), mod3(sc), mod3(gm), s0r, s0i, win, bre, bim, cre, cim,
      ar.reshape(1, ns), ai.reshape(1, ns), dsk.reshape(1, d), wglu)


def _router_kernel(x_ref, g_ref, sh_ref, sc_ref, wrt_ref, br_ref, cin_ref, h_ref, ti_ref, tg_ref, rk_ref, cnt_ref):
    @pl.when((pl.program_id(0) == 0) & (pl.program_id(1) == 0))
    def _():
        cnt_ref[...] = cin_ref[...]

    x = x_ref[0]
    h = _norm_mod(x, g_ref[0], sc_ref[0], sh_ref[0])
    hb = h.astype(BF16)
    h_ref[...] = h.reshape(h_ref.shape)
    h_lo = (h - hb.astype(F32)).astype(BF16)
    ne = br_ref.shape[0]
    both = _dot_nt(wrt_ref[...].reshape(2 * ne, hb.shape[1]), hb)
    lt = both[:ne] + _dot_nt(wrt_ref[0], h_lo) + both[ne:] + br_ref[...]
    idx = lax.broadcasted_iota(jnp.int32, lt.shape, 0)
    vals, ids = [], []
    for _ in range(TOP_K):
        m = jnp.max(lt, axis=0, keepdims=True)
        i = jnp.min(jnp.where(lt == m, idx, ne), axis=0, keepdims=True)
        vals.append(m)
        ids.append(i)
        lt = jnp.where(idx == i, -jnp.inf, lt)
    es = [jnp.exp(v - vals[0]) for v in vals]
    tot = es[0] + es[1] + es[2] + es[3]
    rows = lt.shape[1]
    ti_ref[...] = jnp.concatenate(ids, axis=0)
    tg_ref[0] = jnp.concatenate([e / tot for e in es], axis=0)

    upper = jnp.where(lax.broadcasted_iota(jnp.int32, (rows, rows), 0) < lax.broadcasted_iota(jnp.int32, (rows, rows), 1),
                      1.0, 0.0).astype(BF16)
    ohs = [jnp.where(idx == ids[k], 1.0, 0.0) for k in range(TOP_K)]
    pre_all = _dot(jnp.concatenate(ohs, axis=0).astype(BF16), upper)
    run = cnt_ref[...]
    ranks = []
    for k in range(TOP_K):
        ranks.append(jnp.sum(ohs[k] * (pre_all[k * ne:(k + 1) * ne] + run), axis=0, keepdims=True))
        run = run + jnp.sum(ohs[k], axis=1, keepdims=True)
    cnt_ref[...] = run
    rk_ref[...] = jnp.concatenate(ranks, axis=0).astype(jnp.int32)


def _router(x3, g, sh3, sc3, wrt, br, cnt_in, *, tl):
    nbx, L, d = x3.shape
    lm = sh3.shape[1]
    ne = wrt.shape[1]
    nt = L // tl
    t = nbx * L
    mod_spec = (pl.BlockSpec((1, 1, d), lambda b, c: (b, 0, 0)) if lm == 1
                else pl.BlockSpec((1, tl, d), lambda b, c: (b, c, 0)))
    tok = lambda rows: pl.BlockSpec((rows, tl), lambda b, c: (0, b * nt + c))
    h, ti, tg, rk, cnt = pl.pallas_call(
        _router_kernel,
        grid=(nbx, nt),
        in_specs=[pl.BlockSpec((1, tl, d), lambda b, c: (b, c, 0)),
                  pl.BlockSpec((1, 1, d), lambda b, c: (0, 0, 0)), mod_spec, mod_spec,
                  pl.BlockSpec((2, ne, d), lambda b, c: (0, 0, 0)), pl.BlockSpec((ne, 1), lambda b, c: (0, 0)),
                  pl.BlockSpec((ne, 1), lambda b, c: (0, 0))],
        out_specs=(pl.BlockSpec((tl, d // LANES, LANES), lambda b, c: (b * nt + c, 0, 0)), tok(TOP_K),
                   pl.BlockSpec((1, TOP_K, tl), lambda b, c: (b * nt + c, 0, 0)),
                   tok(TOP_K), pl.BlockSpec((ne, 1), lambda b, c: (0, 0))),
        out_shape=(jax.ShapeDtypeStruct((t, d // LANES, LANES), F32), jax.ShapeDtypeStruct((TOP_K, t), jnp.int32),
                   jax.ShapeDtypeStruct((t // tl, TOP_K, tl), F32), jax.ShapeDtypeStruct((TOP_K, t), jnp.int32),
                   jax.ShapeDtypeStruct((ne, 1), F32)),
        compiler_params=_cparams(("arbitrary", "arbitrary"), 32),
        name="router",
    )(x3, g.reshape(1, 1, d), sh3, sc3, wrt, br, cnt_in)
    return h, ti, tg.reshape(-1), rk, cnt


def _pos_kernel(cnt_ref, ti_ref, rk_ref, pos_ref, *, tm):
    cnt = cnt_ref[...]
    ne = cnt.shape[0]
    tiles = jnp.floor((cnt + (tm - 1)) * (1.0 / tm))
    low = jnp.where(lax.broadcasted_iota(jnp.int32, (ne, ne), 1) < lax.broadcasted_iota(jnp.int32, (ne, ne), 0),
                    1.0, 0.0).astype(BF16)
    start = _dot(low, jnp.broadcast_to(tiles, (ne, LANES)).astype(BF16))[:, 0:1] * tm
    ti = ti_ref[...]
    idx = lax.broadcasted_iota(jnp.int32, (ne, ti.shape[1]), 0)
    sel = [jnp.sum(jnp.where(idx == ti[k:k + 1, :], start, 0.0), axis=0, keepdims=True) for k in range(TOP_K)]
    pos_ref[0] = jnp.concatenate(sel, axis=0).astype(jnp.int32) + rk_ref[...]


def _pos(cnt, ti, rk, *, tm, tl):
    k, t = ti.shape
    ne = cnt.shape[0]
    tok = pl.BlockSpec((k, tl), lambda i: (0, i))
    return pl.pallas_call(
        functools.partial(_pos_kernel, tm=tm),
        grid=(t // tl,),
        in_specs=[pl.BlockSpec((ne, 1), lambda i: (0, 0)), tok, tok],
        out_specs=pl.BlockSpec((1, k, tl), lambda i: (i, 0, 0)),
        out_shape=jax.ShapeDtypeStruct((t // tl, k, tl), jnp.int32),
        compiler_params=_cparams(("arbitrary",), 32),
        name="moe_pos",
    )(cnt, ti, rk).reshape(-1)


def _plan_kernel(cnt_ref, te_ref, tv_ref, tf_ref, *, tm, ne, n_tiles):
    shift = tm.bit_length() - 1

    def body(i, carry):
        e, end = carry

        def adv(c):
            e2 = c[0] + 1
            return e2, c[1] + ((cnt_ref[e2] + (tm - 1)) >> shift)

        e2, end2 = lax.while_loop(lambda c: (c[1] <= i) & (c[0] < ne - 1), adv, (e, end))
        te_ref[i] = e2
        tv_ref[i] = (i < end2).astype(jnp.int32)
        tf_ref[i] = ((e2 != e) | (i == 0)).astype(jnp.int32)
        return e2, end2

    lax.fori_loop(0, n_tiles, body, (jnp.int32(0), (cnt_ref[0] + (tm - 1)) >> shift))


def _plan(cnt_i32, *, tm, n_tiles):
    ne = cnt_i32.shape[0]
    smem = pl.BlockSpec(memory_space=pltpu.SMEM)
    out = jax.ShapeDtypeStruct((n_tiles,), jnp.int32)
    return pl.pallas_call(
        functools.partial(_plan_kernel, tm=tm, ne=ne, n_tiles=n_tiles),
        in_specs=[smem], out_specs=(smem, smem, smem), out_shape=(out, out, out), name="moe_plan",
    )(cnt_i32)


def _dispatch_kernel(pos_ref, h_ref, xs_in, xs_out, sem, *, tl):
    del xs_in

    def body(j, c):
        for u in range(SUBLANES):
            r = j * SUBLANES + u
            for k in range(TOP_K):
                pltpu.make_async_copy(h_ref.at[r], xs_out.at[pos_ref[k * tl + r]], sem).start(priority=k % 2)
        return c

    lax.fori_loop(0, tl // SUBLANES, body, 0)
    for k in range(TOP_K):
        pltpu.make_async_copy(h_ref, xs_out.at[pl.ds(0, tl)], sem).wait()


def _dispatch(pos, h, xs, *, tl):
    t, nj, _ = h.shape
    return pl.pallas_call(
        functools.partial(_dispatch_kernel, tl=tl),
        grid=(t // tl,),
        in_specs=[pl.BlockSpec((TOP_K * tl,), lambda i: (i,), memory_space=pltpu.SMEM),
                  pl.BlockSpec((tl, nj, LANES), lambda i: (i, 0, 0)), pl.BlockSpec(memory_space=pl.ANY)],
        out_specs=pl.BlockSpec(memory_space=pl.ANY),
        out_shape=jax.ShapeDtypeStruct(xs.shape, xs.dtype),
        scratch_shapes=[pltpu.SemaphoreType.DMA],
        input_output_aliases={2: 0},
        compiler_params=_cparams(("arbitrary",), 32),
        name="moe_dispatch",
    )(pos, h, xs)


def _combine_kernel(pos_ref, posn_ref, tg_ref, x_ref, gf_ref, fg_ref, outs_hbm, xo_ref, buf, sem, *,
                    tl, nsteps, final):
    i = pl.program_id(0)
    slot = i % 2
    nslot = 1 - slot
    d = x_ref.shape[2]

    def start_row(pref, s, r):
        for k in range(TOP_K):
            pltpu.make_async_copy(outs_hbm.at[pref[k * tl + r]], buf.at[s, k, r], sem.at[s]).start(priority=k % 2)

    def wait_tile(s):
        for k in range(TOP_K):
            pltpu.make_async_copy(outs_hbm.at[pl.ds(0, tl)], buf.at[s, k], sem.at[s]).wait()

    @pl.when(i == 0)
    def _():
        def body(j, c):
            for u in range(SUBLANES):
                start_row(pos_ref, 0, j * SUBLANES + u)
            return c

        lax.fori_loop(0, tl // SUBLANES, body, 0)

    wait_tile(slot)

    def body(j, c):
        r0 = pl.multiple_of(j * SUBLANES, SUBLANES)
        rows = pl.ds(r0, SUBLANES)
        ys = []
        for u in range(SUBLANES):
            r = r0 + u
            start_row(posn_ref, nslot, r)
            y = tg_ref[r] * buf[slot, 0, r]
            for k in range(1, TOP_K):
                y = y + tg_ref[k * tl + r] * buf[slot, k, r]
            ys.append(y)
        y = jnp.stack(ys, axis=0).reshape(SUBLANES, d)
        gf = gf_ref[0] if gf_ref.shape[1] == 1 else gf_ref[0, rows, :]
        xo_ref[0, rows, :] = x_ref[0, rows, :] + gf * y
        return c

    lax.fori_loop(0, tl // SUBLANES, body, 0)
    if final:
        xn = xo_ref[0]
        xo_ref[0] = xn * lax.rsqrt(jnp.mean(xn * xn, axis=-1, keepdims=True) + EPS) * fg_ref[...]

    @pl.when(i == nsteps - 1)
    def _():
        wait_tile(nslot)


def _combine(pos, x3, gf3, tg, fgain, outs, *, tl, final):
    nbx, L, d = x3.shape
    nt = L // tl
    nsteps = nbx * nt
    nj = d // LANES
    lm = gf3.shape[1]
    mod_spec = (pl.BlockSpec((1, 1, d), lambda i: (i // nt, 0, 0)) if lm == 1
                else pl.BlockSpec((1, tl, d), lambda i: (i // nt, i % nt, 0)))
    return pl.pallas_call(
        functools.partial(_combine_kernel, tl=tl, nsteps=nsteps, final=final),
        grid=(nsteps,),
        in_specs=[pl.BlockSpec((TOP_K * tl,), lambda i: (i,), memory_space=pltpu.SMEM),
                  pl.BlockSpec((TOP_K * tl,), lambda i: (jnp.minimum(i + 1, nsteps - 1),), memory_space=pltpu.SMEM),
                  pl.BlockSpec((TOP_K * tl,), lambda i: (i,), memory_space=pltpu.SMEM),
                  pl.BlockSpec((1, tl, d), lambda i: (i // nt, i % nt, 0)), mod_spec,
                  pl.BlockSpec((1, d), lambda i: (0, 0)), pl.BlockSpec(memory_space=pl.ANY)],
        out_specs=pl.BlockSpec((1, tl, d), lambda i: (i // nt, i % nt, 0)),
        out_shape=jax.ShapeDtypeStruct((nbx, L, d), F32),
        scratch_shapes=[pltpu.VMEM((2, TOP_K, tl, nj, LANES), F32), pltpu.SemaphoreType.DMA((2,))],
        compiler_params=_cparams(("arbitrary",), 48),
        name="moe_combine",
    )(pos, pos, tg, x3, gf3, fgain.reshape(1, d), outs)


def _gmm_kernel(te_ref, tv_ref, tf_ref, x_ref, wgu_ref, bgu_ref, wd_ref, bd_ref, o_ref,
                wgu_s, wd_s, act_s):
    i = pl.program_id(0)
    de = wd_s.shape[0]
    nch = 4
    cw = de // nch

    @pl.when(tf_ref[i] == 1)
    def _():
        for n in range(2 * nch):
            wgu_s[:, n * cw:(n + 1) * cw] = wgu_ref[0, :, n * cw:(n + 1) * cw].astype(BF16)
        for n in range(nch):
            wd_s[n * cw:(n + 1) * cw, :] = wd_ref[0, n * cw:(n + 1) * cw, :].astype(BF16)

    @pl.when(tv_ref[i] == 1)
    def _():
        x = x_ref[...].reshape(act_s.shape[0], wgu_s.shape[0]).astype(BF16)
        for n in range(nch):
            glu = _dot(x, wgu_s[:, n * cw:(n + 1) * cw]) + bgu_ref[0, :, n * cw:(n + 1) * cw]
            lin = _dot(x, wgu_s[:, de + n * cw:de + (n + 1) * cw]) + bgu_ref[0, :, de + n * cw:de + (n + 1) * cw]
            glu = jnp.minimum(glu, SWIGLU_LIMIT)
            lin = jnp.clip(lin, -SWIGLU_LIMIT, SWIGLU_LIMIT)
            act = glu * jax.nn.sigmoid(SWIGLU_ALPHA * glu) * (lin + 1.0)
            act_s[:, n * cw:(n + 1) * cw] = act.astype(BF16)
        o_ref[...] = (_dot(act_s[...], wd_s[...]) + bd_ref[0]).reshape(o_ref.shape)

    @pl.when(tv_ref[i] == 0)
    def _():
        o_ref[...] = jnp.zeros_like(o_ref)


def _gmm(xs, tile_expert, tile_valid, tile_first, w_gu, b_gu, w_down, b_down, *, tm, layer):
    n_slots, nj, _ = xs.shape
    depth, ne, d, de2 = w_gu.shape
    de = de2 // 2
    n_tiles = n_slots // tm
    b_gu = b_gu.reshape(depth * ne, 1, de2)
    b_down = b_down.reshape(depth * ne, 1, d)
    rows = pl.BlockSpec((tm, nj, LANES), lambda i, te, tv, tf: (i, 0, 0))
    grid_spec = pltpu.PrefetchScalarGridSpec(
        num_scalar_prefetch=3,
        grid=(n_tiles,),
        in_specs=[rows,
                  pl.BlockSpec((None, 1, d, de2), lambda i, te, tv, tf: (layer, te[i], 0, 0)),
                  pl.BlockSpec((1, 1, de2), lambda i, te, tv, tf: (layer * ne + te[i], 0, 0)),
                  pl.BlockSpec((None, 1, de, d), lambda i, te, tv, tf: (layer, te[i], 0, 0)),
                  pl.BlockSpec((1, 1, d), lambda i, te, tv, tf: (layer * ne + te[i], 0, 0))],
        out_specs=rows,
        scratch_shapes=[pltpu.VMEM((d, de2), BF16), pltpu.VMEM((de, d), BF16), pltpu.VMEM((tm, de), BF16)],
    )
    return pl.pallas_call(
        _gmm_kernel,
        grid_spec=grid_spec,
        out_shape=jax.ShapeDtypeStruct((n_slots, nj, LANES), F32),
        compiler_params=_cparams(("arbitrary",), 58),
        name="gmm",
    )(tile_expert, tile_valid, tile_first, xs, w_gu, b_gu, w_down, b_down)


def _moe_layer(xp, xs_, g, mods_p, mods_s, wr, br_, w_gu, b_gu, w_down, b_down, slots, fgain, *, layer, final):
    bp, seq, d = xp.shape
    bs = xs_.shape[0]
    ne = wr.shape[1]
    tm = GMM_ROWS
    n_tiles = -(-(TOP_K * (bp * seq + bs)) // tm) + ne
    sh_p, sc_p, gf_p = mods_p
    sh_s, sc_s, gf_s = mods_s
    wr_t = wr.T
    wr_hi = wr_t.astype(BF16)
    wrt = jnp.stack([wr_hi, (wr_t - wr_hi.astype(F32)).astype(BF16)])
    br = br_.reshape(ne, 1)
    cnt0 = jnp.zeros((ne, 1), F32)
    hp, tip, tgp, rkp, cnt1 = _router(xp, g, sh_p[:, None], sc_p[:, None], wrt, br, cnt0, tl=ROUTER_ROWS)
    hs, tis, tgs, rks, cnt = _router(xs_[None], g, sh_s[None], sc_s[None], wrt, br, cnt1, tl=bs)
    pos_p = _pos(cnt, tip, rkp, tm=tm, tl=ROUTER_ROWS)
    pos_s = _pos(cnt, tis, rks, tm=tm, tl=bs)
    te, tv, tf = _plan(cnt.reshape(ne).astype(jnp.int32), tm=tm, n_tiles=n_tiles)
    if slots is None:
        slots = jnp.zeros((n_tiles * tm, d // LANES, LANES), F32)
    slots = _dispatch(pos_p, hp, slots, tl=ROUTER_ROWS)
    slots = _dispatch(pos_s, hs, slots, tl=bs)
    outs = _gmm(slots, te, tv, tf, w_gu, b_gu, w_down, b_down, tm=tm, layer=layer)
    xp = _combine(pos_p, xp, gf_p[:, None], tgp, fgain, outs, tl=ROUTER_ROWS, final=final)
    xs_ = _combine(pos_s, xs_[None], gf_s[None], tgs, fgain, outs, tl=bs, final=final)[0]
    return xp, xs_, slots


def _mlstm_kernel(x_ref, g_ref, sh_ref, sc_ref, gm_ref, wup_ref, cw_ref, cb_ref, wq_ref, wk_ref, wv_ref,
                  wg_ref, bg_ref, lnw_ref, skip_ref, wdn_ref,
                  xo_ref, cst_ref, nst_ref, mst_ref, conv_ref,
                  xm_s, up_s, xc_s, q_s, k_s, v_s, *, lc, heads, dh):
    c = pl.program_id(1)
    inner = heads * dh
    pad = SUBLANES

    @pl.when(c == 0)
    def _():
        cst_ref[...] = jnp.zeros_like(cst_ref)
        nst_ref[...] = jnp.zeros_like(nst_ref)
        mst_ref[...] = jnp.zeros_like(mst_ref)
        xm_s[0:pad, :] = jnp.zeros((pad, inner), F32)

    x = x_ref[0]
    h = _norm_mod(x, g_ref[0], sc_ref[0], sh_ref[0]).astype(BF16)
    up_s[...] = _dot(h, wup_ref[...])
    xm_s[pad:pad + lc, :] = up_s[:, :inner]
    conv = cb_ref[...]
    for k in range(ML_CONV):
        off = pad - (ML_CONV - 1) + k
        conv = conv + cw_ref[k:k + 1, :] * xm_s[off:off + lc, :]
    xc_s[...] = _silu(conv)
    xm_s[0:pad, :] = xm_s[lc:lc + pad, :]
    conv_ref[0] = xm_s[0:pad, :]

    gacc = jnp.broadcast_to(bg_ref[...], (lc, LANES))
    for hd in range(heads):
        hs = slice(hd * dh, (hd + 1) * dh)
        xch = xc_s[:, hs].astype(BF16)
        q = _dot(xch, wq_ref[hd])
        k = _dot(xch, wk_ref[hd]) * (dh ** -0.5)
        v = _dot(up_s[:, hs].astype(BF16), wv_ref[hd])
        qb = q.astype(BF16)
        vb = v.astype(BF16)
        q_s[:, hs] = qb
        k_s[:, hs] = k
        v_s[:, hs] = vb
        gacc = gacc + _dot(qb, wg_ref[0, hd]) + _dot(k.astype(BF16), wg_ref[1, hd]) + _dot(vb, wg_ref[2, hd])

    lf = _log_sigmoid(gacc)
    row = lax.broadcasted_iota(jnp.int32, (lc, lc), 0)
    col = lax.broadcasted_iota(jnp.int32, (lc, lc), 1)
    causal = row >= col
    tri = jnp.where(causal, 1.0, 0.0).astype(BF16)
    hi = lf.astype(BF16)
    r1 = lf - hi.astype(F32)
    mid = r1.astype(BF16)
    lo = (r1 - mid.astype(F32)).astype(BF16)
    bcum = _dot(tri, hi) + _dot(tri, mid) + _dot(tri, lo)
    g_t = gacc.T
    b_t = bcum.T

    acc = jnp.zeros((lc, xo_ref.shape[2]), F32)
    for hd in range(heads):
        hs = slice(hd * dh, (hd + 1) * dh)
        ig_c = gacc[:, hd:hd + 1]
        b_c = bcum[:, heads + hd:heads + hd + 1]
        ig_r = g_t[hd:hd + 1, :]
        b_r = b_t[heads + hd:heads + hd + 1, :]
        m_prev = mst_ref[0, hd][:, 0:1]
        dm = jnp.where(causal, (b_c - b_r) + ig_r, -jnp.inf)
        inter = b_c + m_prev
        m_t = jnp.maximum(inter, jnp.max(dm, axis=1, keepdims=True))
        w_intra = jnp.exp(dm - m_t)
        w_inter = jnp.exp(inter - m_t)
        qb = q_s[:, hs]
        kf = k_s[:, hs]
        vb = v_s[:, hs]
        cmat = cst_ref[0, hd]
        nvec = nst_ref[0, hd]
        qk = _dot_nt(qb, kf.astype(BF16)) * w_intra
        num = _dot(qk.astype(BF16), vb) + w_inter * _dot(qb, cmat.astype(BF16))
        qn = jnp.sum(qb.astype(F32) * nvec.astype(BF16).astype(F32), axis=1, keepdims=True)
        den = jnp.sum(qk, axis=1, keepdims=True) + w_inter * qn
        hh = num / jnp.maximum(jnp.abs(den), jnp.exp(-m_t))

        bl = b_c[lc - 1:lc, :]
        m_new = m_t[lc - 1:lc, :]
        w_s = jnp.exp(bl - b_c + ig_c - m_new)
        decay = jnp.exp(bl + m_prev - m_new)
        kw = w_s * kf
        cst_ref[0, hd] = decay * cmat + _dot(kw.T.astype(BF16), vb)
        nst_ref[0, hd] = decay * nvec + jnp.sum(kw, axis=0, keepdims=True)
        mst_ref[0, hd] = jnp.broadcast_to(m_new, (1, LANES))

        mu = jnp.mean(hh, axis=1, keepdims=True)
        hc = hh - mu
        var = jnp.mean(hc * hc, axis=1, keepdims=True)
        hn = hc * lax.rsqrt(var + EPS) * lnw_ref[:, hs]
        o = (hn + skip_ref[:, hs] * xc_s[:, hs]) * _silu(up_s[:, inner + hd * dh:inner + (hd + 1) * dh])
        acc = acc + _dot(o.astype(BF16), wdn_ref[hs, :])
    xo_ref[0] = x + gm_ref[0] * acc


def _mlstm_prompt(x, g, sh, sc, gm, wts, *, lc):
    wup, cw, cb, wq, wk, wv, wg, bg, lnw, skip, wdn = wts
    nb, L, d = x.shape
    heads, dh, _ = wq.shape
    inner = heads * dh
    const = lambda shape: pl.BlockSpec(shape, lambda b, c: (0,) * len(shape), pipeline_mode=pl.Buffered(1))
    per_b = lambda shape: pl.BlockSpec(shape, lambda b, c: (b,) + (0,) * (len(shape) - 1))
    mod3 = lambda t: t.reshape(nb, 1, d)
    kern = functools.partial(_mlstm_kernel, lc=lc, heads=heads, dh=dh)
    return pl.pallas_call(
        kern,
        grid=(nb, L // lc),
        in_specs=[pl.BlockSpec((1, lc, d), lambda b, c: (b, c, 0)), const((1, 1, d)),
                  per_b((1, 1, d)), per_b((1, 1, d)), per_b((1, 1, d)),
                  const(wup.shape), const(cw.shape), const((1, inner)), const(wq.shape), const(wk.shape),
                  const(wv.shape), const(wg.shape), const((1, LANES)), const((1, inner)), const((1, inner)),
                  const(wdn.shape)],
        out_specs=(pl.BlockSpec((1, lc, d), lambda b, c: (b, c, 0)),
                   per_b((1, heads, dh, dh)), per_b((1, heads, 1, dh)), per_b((1, heads, 1, LANES)),
                   per_b((1, SUBLANES, inner))),
        out_shape=(jax.ShapeDtypeStruct((nb, L, d), F32),
                   jax.ShapeDtypeStruct((nb, heads, dh, dh), F32),
                   jax.ShapeDtypeStruct((nb, heads, 1, dh), F32),
                   jax.ShapeDtypeStruct((nb, heads, 1, LANES), F32),
                   jax.ShapeDtypeStruct((nb, SUBLANES, inner), F32)),
        scratch_shapes=[pltpu.VMEM((lc + SUBLANES, inner), F32), pltpu.VMEM((lc, 2 * inner), F32),
                        pltpu.VMEM((lc, inner), F32), pltpu.VMEM((lc, inner), BF16),
                        pltpu.VMEM((lc, inner), F32), pltpu.VMEM((lc, inner), BF16)],
        compiler_params=_cparams(("arbitrary", "arbitrary"), 58),
        name="mlstm",
    )(x, g.reshape(1, 1, d), mod3(sh), mod3(sc), mod3(gm), wup, cw, cb.reshape(1, inner), wq, wk, wv, wg, bg,
      lnw.reshape(1, inner), skip.reshape(1, inner), wdn)


def _mls_proj_kernel(x_ref, g_ref, sh_ref, sc_ref, wxm_ref, wz_ref, buf_ref, cw_ref, cb_ref, wq_ref, wk_ref,
                     wv_ref, wg_ref, bg_ref,
                     q_ref, k_ref, v_ref, xc_ref, z_ref, gate_ref, nbuf_ref):
    hd = pl.program_id(0)

    @pl.when(hd == 0)
    def _():
        gate_ref[...] = jnp.broadcast_to(bg_ref[...], gate_ref.shape)

    h = _norm_mod(x_ref[...], g_ref[...], sc_ref[...], sh_ref[...]).astype(BF16)
    xm = _dot(h, wxm_ref[...])
    z_ref[...] = _dot(h, wz_ref[...])
    conv = cb_ref[...] + cw_ref[ML_CONV - 1:ML_CONV, :] * xm
    for k in range(ML_CONV - 1):
        conv = conv + cw_ref[k:k + 1, :] * buf_ref[k]
    for k in range(ML_CONV - 2):
        nbuf_ref[k] = buf_ref[k + 1]
    nbuf_ref[ML_CONV - 2] = xm
    xc = _silu(conv)
    xc_ref[...] = xc
    xcb = xc.astype(BF16)
    dh = xm.shape[1]
    q = _dot(xcb, wq_ref[0])
    k = _dot(xcb, wk_ref[0]) * (dh ** -0.5)
    v = _dot(xm.astype(BF16), wv_ref[0])
    q_ref[...] = q
    k_ref[...] = k
    v_ref[...] = v
    gate_ref[...] += (_dot(q.astype(BF16), wg_ref[0, 0]) + _dot(k.astype(BF16), wg_ref[1, 0])
                      + _dot(v.astype(BF16), wg_ref[2, 0]))


def _mls_proj(x, g, sh, sc, buf, wts):
    wup, cw, cb, wq, wk, wv, wg, bg = wts
    nb, d = x.shape
    heads, dh, _ = wq.shape
    inner = heads * dh
    full = lambda shape: pl.BlockSpec(shape, lambda h: (0,) * len(shape))
    colblk = pl.BlockSpec((nb, dh), lambda h: (0, h))
    act = jax.ShapeDtypeStruct((nb, inner), F32)
    return pl.pallas_call(
        _mls_proj_kernel,
        grid=(heads,),
        in_specs=[full((nb, d)), full((1, d)), full((nb, d)), full((nb, d)),
                  pl.BlockSpec((d, dh), lambda h: (0, h)), pl.BlockSpec((d, dh), lambda h: (0, heads + h)),
                  pl.BlockSpec((ML_CONV - 1, nb, dh), lambda h: (0, 0, h)),
                  pl.BlockSpec((ML_CONV, dh), lambda h: (0, h)), pl.BlockSpec((1, dh), lambda h: (0, h)),
                  pl.BlockSpec((1, dh, dh), lambda h: (h, 0, 0)), pl.BlockSpec((1, dh, dh), lambda h: (h, 0, 0)),
                  pl.BlockSpec((1, dh, dh), lambda h: (h, 0, 0)),
                  pl.BlockSpec((3, 1, dh, LANES), lambda h: (0, h, 0, 0)), full((1, LANES))],
        out_specs=(colblk, colblk, colblk, colblk, colblk, full((nb, LANES)),
                   pl.BlockSpec((ML_CONV - 1, nb, dh), lambda h: (0, 0, h))),
        out_shape=(act, act, act, act, act, jax.ShapeDtypeStruct((nb, LANES), F32),
                   jax.ShapeDtypeStruct((ML_CONV - 1, nb, inner), F32)),
        compiler_params=_cparams(("arbitrary",), 32),
        name="mls_proj",
    )(x, g.reshape(1, d), sh, sc, wup, wup, buf, cw, cb.reshape(1, inner), wq, wk, wv, wg, bg)


def _mls_gate_scalars(ig, fg, m0):
    lf = _log_sigmoid(fg)
    inter = lf + m0
    m_t = jnp.maximum(inter, ig)
    return m_t, jnp.exp(ig - m_t), jnp.exp(inter - m_t)


def _mls_state_kernel(q_ref, k_ref, v_ref, ig_ref, fg_ref, m0_ref, c_ref, qc_ref, cn_ref, *, heads, dh):
    _, w_in, w_dec = _mls_gate_scalars(ig_ref[0], fg_ref[0], m0_ref[0])
    rowmask = lax.broadcasted_iota(jnp.int32, (LANES, dh), 0) == 0
    for hd in range(heads):
        hs = slice(hd * dh, (hd + 1) * dh)
        wi = w_in[:, hd:hd + 1]
        wd = w_dec[:, hd:hd + 1]
        cmat = c_ref[0, hd]
        q8 = jnp.broadcast_to(q_ref[0, :, hs], (SUBLANES, dh)).astype(BF16)
        qc_ref[0, :, hs] = _dot(q8, cmat.astype(BF16))[0:1, :]
        kw = jnp.where(rowmask, wi * k_ref[0, :, hs], 0.0)
        vv = jnp.where(rowmask, v_ref[0, :, hs], 0.0)
        cn_ref[0, hd] = wd * cmat + _dot(kw.T.astype(BF16), vv.astype(BF16))


def _mls_state(q, k, v, ig, fg, m0, cst):
    nb, inner = q.shape
    _, heads, dh, _ = cst.shape
    row = lambda w: pl.BlockSpec((1, 1, w), lambda b: (b, 0, 0))
    r3 = lambda t: t.reshape(nb, 1, t.shape[1])
    kern = functools.partial(_mls_state_kernel, heads=heads, dh=dh)
    qc, cn = pl.pallas_call(
        kern,
        grid=(nb,),
        in_specs=[row(inner), row(inner), row(inner), row(LANES), row(LANES), row(LANES),
                  pl.BlockSpec((1, heads, dh, dh), lambda b: (b, 0, 0, 0))],
        out_specs=(row(inner), pl.BlockSpec((1, heads, dh, dh), lambda b: (b, 0, 0, 0))),
        out_shape=(jax.ShapeDtypeStruct((nb, 1, inner), F32), jax.ShapeDtypeStruct(cst.shape, F32)),
        compiler_params=_cparams(("arbitrary",), 40),
        name="mls_state",
    )(r3(q), r3(k), r3(v), r3(ig), r3(fg), r3(m0), cst)
    return qc.reshape(nb, inner), cn


def _mls_post_kernel(x_ref, gm_ref, q_ref, k_ref, v_ref, xc_ref, z_ref, qc_ref, n_ref, ig_ref, fg_ref, m0_ref,
                     lnw_ref, skip_ref, wdn_ref, xo_ref, nn_ref, mn_ref, *, heads, dh):
    m_t, w_in, w_dec = _mls_gate_scalars(ig_ref[...], fg_ref[...], m0_ref[...])
    mn_ref[...] = m_t
    acc = jnp.zeros(xo_ref.shape, F32)
    for hd in range(heads):
        hs = slice(hd * dh, (hd + 1) * dh)
        wi = w_in[:, hd:hd + 1]
        wd = w_dec[:, hd:hd + 1]
        mt = m_t[:, hd:hd + 1]
        q = q_ref[:, hs]
        k = k_ref[:, hs]
        qr = q.astype(BF16).astype(F32)
        qk = jnp.sum(qr * k.astype(BF16).astype(F32), axis=1, keepdims=True) * wi
        nvec = n_ref[:, hs]
        num = qk.astype(BF16).astype(F32) * v_ref[:, hs].astype(BF16).astype(F32) + wd * qc_ref[:, hs]
        den = qk + wd * jnp.sum(qr * nvec.astype(BF16).astype(F32), axis=1, keepdims=True)
        hh = num / jnp.maximum(jnp.abs(den), jnp.exp(-mt))
        nn_ref[:, hs] = wd * nvec + wi * k
        mu = jnp.mean(hh, axis=1, keepdims=True)
        hc = hh - mu
        var = jnp.mean(hc * hc, axis=1, keepdims=True)
        hn = hc * lax.rsqrt(var + EPS) * lnw_ref[:, hs]
        o = (hn + skip_ref[:, hs] * xc_ref[:, hs]) * _silu(z_ref[:, hs])
        acc = acc + _dot(o.astype(BF16), wdn_ref[hs, :])
    xo_ref[...] = x_ref[...] + gm_ref[...] * acc


def _mls_post(x, gm, q, k, v, xc, z, qc, n0, ig, fg, m0, lnw, skip, wdn, *, heads, dh):
    nb, d = x.shape
    inner = heads * dh
    kern = functools.partial(_mls_post_kernel, heads=heads, dh=dh)
    return pl.pallas_call(
        kern,
        out_shape=(jax.ShapeDtypeStruct((nb, d), F32), jax.ShapeDtypeStruct((nb, inner), F32),
                   jax.ShapeDtypeStruct((nb, LANES), F32)),
        compiler_params=pltpu.CompilerParams(vmem_limit_bytes=40 * MIB),
        name="mls_post",
    )(x, gm, q, k, v, xc, z, qc, n0, ig, fg, m0, lnw.reshape(1, inner), skip.reshape(1, inner), wdn)


def _pad_lanes(t):
    return jnp.pad(t, ((0, 0), (0, LANES - t.shape[1])))


def kernel(x_prompt, x_sample, c_prompt, c_sample, state_s5_re, state_s5_im, state_mlstm_C, state_mlstm_n, state_mlstm_m, state_mlstm_conv, norm_mix_g, norm_ffn_g, final_norm_g, ada_w, ada_b, s5_w_in, s5_lam_re, s5_lam_im, s5_log_dt, s5_b_re, s5_b_im, s5_c_re, s5_c_im, s5_d, s5_w_glu, ml_w_up, ml_conv_w, ml_conv_b, ml_w_q, ml_w_k, ml_w_v, ml_w_gate, ml_b_gate, ml_ln_w, ml_skip, ml_w_down, moe_w_router, moe_b_router, moe_w_gu, moe_b_gu, moe_w_down, moe_b_down):
    bp, seq, d = x_prompt.shape
    bs = x_sample.shape[0]
    tp = bp * seq
    _, groups, pstate, gch = s5_b_re.shape
    ns = groups * pstate
    heads, dh = ml_w_q.shape[1], ml_w_q.shape[2]
    inner = heads * dh
    ne = moe_w_router.shape[2]

    mods = _ada(jnp.concatenate([c_prompt, c_sample], axis=0), ada_w, ada_b)

    def mod(i, j):
        m = mods[i, :, j * d:(j + 1) * d]
        return m[:bp], m[bp:]

    ar, ai, bbr, bbi = _s5_prep(s5_lam_re[0], s5_lam_im[0], s5_log_dt[0], s5_b_re[0], s5_b_im[0])
    nblk = d // LANES
    bre = _block_diag(bbr.reshape(groups, gch, pstate), nblk).astype(BF16)
    bim = _block_diag(bbi.reshape(groups, gch, pstate), nblk).astype(BF16)
    cre = _block_diag(s5_c_re[0].transpose(0, 2, 1), nblk).astype(BF16)
    cim = _block_diag(s5_c_im[0].transpose(0, 2, 1), nblk).astype(BF16)
    s5_wts = (s5_w_in[0].astype(BF16), bre, bim, cre, cim, ar.reshape(ns), ai.reshape(ns), s5_d[0],
              s5_w_glu[0].astype(BF16))
    (sh_p, sh_s), (sc_p, sc_s), (gm_p, gm_s) = mod(0, 0), mod(0, 1), mod(0, 2)
    zeros_st = jnp.zeros((bp, ns), F32)
    xp, p_re, p_im = _s5_layer(x_prompt, norm_mix_g[0], sh_p, sc_p, gm_p, zeros_st, zeros_st, s5_wts,
                               batch_major=True, lc=S5_CHUNK)
    xs_, s_re, s_im = _s5_layer(x_sample.reshape(bs, d), norm_mix_g[0], sh_s, sc_s, gm_s,
                                state_s5_re[0].reshape(bs, ns), state_s5_im[0].reshape(bs, ns), s5_wts,
                                batch_major=False, lc=1)

    def moe_layer(i, xp, xs_, slots, final):
        (sh_p, sh_s), (sc_p, sc_s), (gf_p, gf_s) = mod(i, 3), mod(i, 4), mod(i, 5)
        return _moe_layer(xp, xs_, norm_ffn_g[i], (sh_p, sc_p, gf_p), (sh_s, sc_s, gf_s), moe_w_router[i],
                          moe_b_router[i], moe_w_gu, moe_b_gu, moe_w_down, moe_b_down, slots,
                          final_norm_g, layer=i, final=final)

    xp, xs_, slots = moe_layer(0, xp, xs_, None, False)

    wg = _pad_lanes(ml_w_gate[0]).reshape(3, heads, dh, LANES).astype(BF16)
    bg = _pad_lanes(ml_b_gate[0][None])
    wup = ml_w_up[0].astype(BF16)
    wq, wk, wv = ml_w_q[0].astype(BF16), ml_w_k[0].astype(BF16), ml_w_v[0].astype(BF16)
    wdn = ml_w_down[0].astype(BF16)
    (sh_p, sh_s), (sc_p, sc_s), (gm_p, gm_s) = mod(1, 0), mod(1, 1), mod(1, 2)
    xp, p_c, p_n, p_m, p_tail = _mlstm_prompt(
        xp, norm_mix_g[1], sh_p, sc_p, gm_p,
        (wup, ml_conv_w[0], ml_conv_b[0], wq, wk, wv, wg, bg, ml_ln_w[0], ml_skip[0], wdn), lc=ML_CHUNK)
    p_n = p_n.reshape(bp, heads, dh)
    p_m = p_m[:, :, 0, 0]
    p_conv = p_tail[:, SUBLANES - (ML_CONV - 1):, :]

    buf = state_mlstm_conv[0].transpose(1, 0, 2)
    q, k, v, xc, z, gates, nbuf = _mls_proj(xs_, norm_mix_g[1], sh_s, sc_s, buf,
                                            (wup, ml_conv_w[0], ml_conv_b[0], wq, wk, wv, wg, bg))
    ig = _pad_lanes(gates[:, :heads])
    fg = _pad_lanes(gates[:, heads:2 * heads])
    m0 = _pad_lanes(state_mlstm_m[0])
    qc, s_c = _mls_state(q, k, v, ig, fg, m0, state_mlstm_C[0])
    xs_, s_n, s_m = _mls_post(xs_, gm_s, q, k, v, xc, z, qc, state_mlstm_n[0].reshape(bs, inner), ig, fg, m0,
                              ml_ln_w[0], ml_skip[0], wdn, heads=heads, dh=dh)
    s_n = s_n.reshape(bs, heads, dh)
    s_m = s_m[:, :heads]
    s_conv = nbuf.transpose(1, 0, 2)

    y_p, y_s, _ = moe_layer(1, xp, xs_, slots, True)
    y_s = y_s.reshape(bs, 1, d)
    return (y_p, y_s,
            p_re.reshape(1, bp, groups, pstate), p_im.reshape(1, bp, groups, pstate),
            p_c[None], p_n[None], p_m[None], p_conv[None],
            s_re.reshape(1, bs, groups, pstate), s_im.reshape(1, bs, groups, pstate),
            s_c[None], s_n[None], s_m[None], s_conv[None])
```
